```python
import jax, jax.numpy as jnp
from jax import lax
import numpy as np

D_MODEL = 1024
BATCH = 32
SEQ = 256
DEPTH = 4
DEC_BATCH = 2
DEC_SEQ = 4096
PAST_LEN = 256

GRID_W = 64
HEAD_DIM = 64
GQA_HEADS = 6
GQA_KV_HEADS = 2
GQA_GROUP = GQA_HEADS // GQA_KV_HEADS
GQA_W = GQA_HEADS * HEAD_DIM
MLSTM_HEADS = 4
MLSTM_DK = 64
MLSTM_DV = 64
MLSTM_W = MLSTM_HEADS * MLSTM_DV
MLSTM_CHUNK = 64
MLA_HEADS = 6
MLA_Q_RANK = 256
MLA_KV_RANK = 256
MLA_NOPE = 64
MLA_ROPE = 32
MLA_V = 64
MLA_QK = MLA_NOPE + MLA_ROPE
MLA_W = MLA_HEADS * MLA_V
MIX_W = GQA_W + MLSTM_W + MLA_W
D_FF = 2816
Q_BLOCK = 128
ROPE_BASE = 10000.0
EPS = 1e-6
N_MOD = 9
IN_SPLITS = (GQA_W, GQA_KV_HEADS * HEAD_DIM, GQA_KV_HEADS * HEAD_DIM,
             MLSTM_HEADS * MLSTM_DK, MLSTM_HEADS * MLSTM_DK, MLSTM_W, MLSTM_W, 4 * MLSTM_HEADS,
             MLA_Q_RANK, MLA_KV_RANK, MLA_ROPE)
D_IN = sum(IN_SPLITS)

kernel_name = 'hybrid_gqa_mlstm_mla_dit_step'


def rms_norm(x, w):
    xf = x.astype(jnp.float32)
    y = xf * lax.rsqrt(jnp.mean(xf * xf, axis=-1, keepdims=True) + EPS)
    return (y * w).astype(x.dtype)


def swiglu(h, wg, wu, wd):
    return (jax.nn.silu(h @ wg) * (h @ wu)) @ wd


def axial_rope(rows, rot_dim):
    half = rot_dim // 2
    freqs = ROPE_BASE ** (-jnp.arange(0, half, 2, dtype=jnp.float32) / half)
    r = jnp.repeat(jnp.arange(rows, dtype=jnp.float32), GRID_W)
    c = jnp.tile(jnp.arange(GRID_W, dtype=jnp.float32), rows)
    ang = jnp.concatenate([r[:, None] * freqs, c[:, None] * freqs], axis=-1)
    return jnp.cos(ang), jnp.sin(ang)


def apply_rope(x, cos, sin):
    half = x.shape[-1] // 2
    x1, x2 = x[..., :half], x[..., half:]
    c, s = cos[None, :, None, :], sin[None, :, None, :]
    return jnp.concatenate([x1 * c - x2 * s, x1 * s + x2 * c], axis=-1).astype(x.dtype)


def block_attention(q, k, v, scale):
    B, T, KVH, G, Dq = q.shape
    nb = T // Q_BLOCK
    qb = jnp.moveaxis(q.reshape(B, nb, Q_BLOCK, KVH, G, Dq), 1, 0)

    def one_block(qblk):
        s = jnp.einsum('bqhgd,bkhd->bhgqk', qblk, k, preferred_element_type=jnp.float32) * scale
        p = jax.nn.softmax(s, axis=-1)
        return jnp.einsum('bhgqk,bkhd->bqhgd', p.astype(v.dtype), v)

    out = lax.map(one_block, qb)
    return jnp.moveaxis(out, 0, 1).reshape(B, T, KVH, G, v.shape[-1])


def mlstm_chunkwise(q, k, v, log_i, log_f, C0, n0, m0):
    B, H, T, _ = q.shape
    L = MLSTM_CHUNK
    nc = T // L

    def to_chunks(a):
        return jnp.moveaxis(a.reshape(B, H, nc, L, *a.shape[3:]), 2, 0)

    xs = tuple(to_chunks(a) for a in (q, k, v, log_i, log_f))
    causal = jnp.tril(jnp.ones((L, L), dtype=bool))

    def step(carry, inp):
        C, n, m = carry
        qc, kc, vc, li, lf = inp
        b = jnp.cumsum(lf, axis=-1)
        D = jnp.where(causal, b[..., :, None] - b[..., None, :] + li[..., None, :], -jnp.inf)
        inter = b + m[..., None]
        m_t = jnp.maximum(inter, jnp.max(D, axis=-1))
        w_inter = jnp.exp(inter - m_t)
        qk = jnp.einsum('bhtd,bhsd->bhts', qc, kc) * jnp.exp(D - m_t[..., None])
        num = w_inter[..., None] * jnp.einsum('bhtd,bhde->bhte', qc, C) + jnp.einsum('bhts,bhse->bhte', qk, vc)
        den = w_inter * jnp.einsum('bhtd,bhd->bht', qc, n) + jnp.sum(qk, axis=-1)
        h = num / jnp.maximum(jnp.abs(den), jnp.exp(-m_t))[..., None]
        bL = b[..., -1]
        decay = bL[..., None] - b + li
        m_new = jnp.maximum(bL + m, jnp.max(decay, axis=-1))
        w_old = jnp.exp(bL + m - m_new)
        w_s = jnp.exp(decay - m_new[..., None])
        C_new = w_old[..., None, None] * C + jnp.einsum('bhs,bhsd,bhse->bhde', w_s, kc, vc)
        n_new = w_old[..., None] * n + jnp.einsum('bhs,bhsd->bhd', w_s, kc)
        return (C_new, n_new, m_new), h

    (C, n, m), hs = lax.scan(step, (C0, n0, m0), xs)
    h = jnp.moveaxis(hs, 0, 2).reshape(B, H, T, v.shape[-1])
    return h, C, n, m


def maybe_flip(a, rev):
    return jnp.flip(a, axis=2) if rev else a


def mlstm_bidir(q, k, v, gates, C0, n0, m0):
    hs, Cs, ns, ms = [], [], [], []
    for d in range(2):
        rev = d == 1
        h, C, n, m = mlstm_chunkwise(
            maybe_flip(q, rev), maybe_flip(k, rev), maybe_flip(v, rev),
            maybe_flip(gates[..., 2 * d], rev),
            jax.nn.log_sigmoid(maybe_flip(gates[..., 2 * d + 1], rev)),
            C0[:, d], n0[:, d], m0[:, d])
        hs.append(maybe_flip(h, rev))
        Cs.append(C)
        ns.append(n)
        ms.append(m)
    return hs[0] + hs[1], jnp.stack(Cs, axis=1), jnp.stack(ns, axis=1), jnp.stack(ms, axis=1)


def mixer_inputs(h, lp):
    B, T, _ = h.shape
    idx = np.cumsum(IN_SPLITS)[:-1].tolist()
    gq, gk, gv, mq, mk, mv, mo, mg, qlat, kvlat, krope = jnp.split(h @ lp['w_in'], idx, axis=-1)
    gqa_q = rms_norm(gq.reshape(B, T, GQA_HEADS, HEAD_DIM), lp['gqa_q_norm'])
    gqa_k = rms_norm(gk.reshape(B, T, GQA_KV_HEADS, HEAD_DIM), lp['gqa_k_norm'])
    gqa_v = gv.reshape(B, T, GQA_KV_HEADS, HEAD_DIM)

    def heads(a, d):
        return a.reshape(B, T, MLSTM_HEADS, d).transpose(0, 2, 1, 3).astype(jnp.float32)

    ml_q = heads(mq, MLSTM_DK) * (MLSTM_DK ** -0.5)
    ml_k = heads(mk, MLSTM_DK)
    ml_v = heads(mv, MLSTM_DV)
    ml_g = (mg.reshape(B, T, 4, MLSTM_HEADS) + lp['mlstm_gate_b']).transpose(0, 3, 1, 2).astype(jnp.float32)
    mla_q = (rms_norm(qlat, lp['mla_q_norm']) @ lp['mla_w_uq']).reshape(B, T, MLA_HEADS, MLA_QK)
    mla_ckv = rms_norm(kvlat, lp['mla_kv_norm'])
    return gqa_q, gqa_k, gqa_v, ml_q, ml_k, ml_v, mo, ml_g, mla_q, mla_ckv, krope


def mla_attend(q, ckv, krope, w_ukv):
    B, S, _ = ckv.shape
    kv = (ckv @ w_ukv).reshape(B, S, MLA_HEADS, MLA_NOPE + MLA_V)
    k_nope, v = kv[..., :MLA_NOPE], kv[..., MLA_NOPE:]
    k = jnp.concatenate([k_nope, jnp.broadcast_to(krope[:, :, None, :], (B, S, MLA_HEADS, MLA_ROPE))], axis=-1)
    return block_attention(q[:, :, :, None, :], k, v, MLA_QK ** -0.5)


def mixer_output(gqa_o, ml_h, ml_o, mla_o, lp):
    B, T = gqa_o.shape[:2]
    hn = rms_norm(ml_h, lp['mlstm_out_norm'].reshape(MLSTM_HEADS, 1, MLSTM_DV))
    ml = hn.transpose(0, 2, 1, 3).reshape(B, T, MLSTM_W).astype(ml_o.dtype) * jax.nn.sigmoid(ml_o)
    cat = jnp.concatenate([gqa_o.reshape(B, T, GQA_W), ml, mla_o.reshape(B, T, MLA_W)], axis=-1)
    return cat @ lp['w_out']


def mix_context(h, lp):
    B, T, _ = h.shape
    gqa_q, gqa_k, gqa_v, ml_q, ml_k, ml_v, ml_o, ml_g, mla_q, mla_ckv, mla_kr = mixer_inputs(h, lp)
    gqa_o = block_attention(gqa_q.reshape(B, T, GQA_KV_HEADS, GQA_GROUP, HEAD_DIM), gqa_k, gqa_v, HEAD_DIM ** -0.5)
    C0 = jnp.zeros((B, 2, MLSTM_HEADS, MLSTM_DK, MLSTM_DV), jnp.float32)
    n0 = jnp.zeros((B, 2, MLSTM_HEADS, MLSTM_DK), jnp.float32)
    m0 = jnp.zeros((B, 2, MLSTM_HEADS), jnp.float32)
    ml_h, C, n, m = mlstm_bidir(ml_q, ml_k, ml_v, ml_g, C0, n0, m0)
    mla_o = mla_attend(mla_q, mla_ckv, mla_kr, lp['mla_w_ukv'])
    return mixer_output(gqa_o, ml_h, ml_o, mla_o, lp), (gqa_k, gqa_v, mla_ckv, mla_kr, C, n, m)


def mix_latent(h, lp, ctx, rope_hd, rope_mla):
    B, T, _ = h.shape
    ck, cv, cckv, ckr, C0, n0, m0 = ctx
    gqa_q, gqa_k, gqa_v, ml_q, ml_k, ml_v, ml_o, ml_g, mla_q, mla_ckv, mla_kr = mixer_inputs(h, lp)
    q = apply_rope(gqa_q, *rope_hd).reshape(B, T, GQA_KV_HEADS, GQA_GROUP, HEAD_DIM)
    k = jnp.concatenate([ck, apply_rope(gqa_k, *rope_hd)], axis=1)
    v = jnp.concatenate([cv, gqa_v], axis=1)
    gqa_o = block_attention(q, k, v, HEAD_DIM ** -0.5)
    ml_h, _, _, _ = mlstm_bidir(ml_q, ml_k, ml_v, ml_g,
                                C0.astype(jnp.float32), n0.astype(jnp.float32), m0.astype(jnp.float32))
    q_mla = jnp.concatenate([mla_q[..., :MLA_NOPE], apply_rope(mla_q[..., MLA_NOPE:], *rope_mla)], axis=-1)
    kr = apply_rope(mla_kr[:, :, None, :], *rope_mla)[:, :, 0]
    mla_o = mla_attend(q_mla, jnp.concatenate([cckv, mla_ckv], axis=1),
                       jnp.concatenate([ckr, kr], axis=1), lp['mla_w_ukv'])
    return mixer_output(gqa_o, ml_h, ml_o, mla_o, lp)


def trunk_layer(x, cond, lp, mix):
    mods = jnp.split(jax.nn.silu(cond) @ lp['w_ada'] + lp['b_ada'], N_MOD, axis=-1)
    sh1, sc1, g1, sh2, sc2, g2, sh3, sc3, g3 = [mm[:, None, :] for mm in mods]
    h = rms_norm(x, lp['norm_w'][0]) * (1 + sc1) + sh1
    x = x + 0.5 * g1 * swiglu(h, lp['ffn_w_gate'][0], lp['ffn_w_up'][0], lp['ffn_w_down'][0])
    h = rms_norm(x, lp['norm_w'][1]) * (1 + sc2) + sh2
    out, extra = mix(h)
    x = x + g2 * out
    h = rms_norm(x, lp['norm_w'][2]) * (1 + sc3) + sh3
    x = x + 0.5 * g3 * swiglu(h, lp['ffn_w_gate'][1], lp['ffn_w_up'][1], lp['ffn_w_down'][1])
    return x, extra


def setup_inputs(seed: int = 0) -> dict:
    key = jax.random.key(seed)
    ks = jax.random.split(key, 32)

    def nrm(k, shape, s=1.0):
        return s * jax.random.normal(k, shape, jnp.float32)

    return {
        'x_prompt': nrm(ks[0], (BATCH, SEQ, D_MODEL)),
        'x_sample': nrm(ks[1], (DEC_BATCH, DEC_SEQ, D_MODEL)),
        'c': nrm(ks[2], (DEC_BATCH, D_MODEL)),
        'cache_gqa_k': nrm(ks[3], (DEC_BATCH, DEPTH, PAST_LEN, GQA_KV_HEADS, HEAD_DIM)),
        'cache_gqa_v': nrm(ks[4], (DEC_BATCH, DEPTH, PAST_LEN, GQA_KV_HEADS, HEAD_DIM)),
        'cache_mla_ckv': nrm(ks[5], (DEC_BATCH, DEPTH, PAST_LEN, MLA_KV_RANK)),
        'cache_mla_krope': nrm(ks[6], (DEC_BATCH, DEPTH, PAST_LEN, MLA_ROPE)),
        'state_mlstm_C': nrm(ks[7], (DEC_BATCH, DEPTH, 2, MLSTM_HEADS, MLSTM_DK, MLSTM_DV)),
        'state_mlstm_n': nrm(ks[8], (DEC_BATCH, DEPTH, 2, MLSTM_HEADS, MLSTM_DK)),
        'state_mlstm_m': jax.random.uniform(ks[9], (DEC_BATCH, DEPTH, 2, MLSTM_HEADS), jnp.float32, 0.0, 3.0),
        'c_ctx': nrm(ks[10], (D_MODEL,)),
        'w_ada': nrm(ks[11], (DEPTH, D_MODEL, N_MOD * D_MODEL), 0.3 * D_MODEL ** -0.5),
        'b_ada': nrm(ks[12], (DEPTH, N_MOD * D_MODEL), 0.02),
        'norm_w': 1.0 + nrm(ks[13], (DEPTH, 3, D_MODEL), 0.05),
        'ffn_w_gate': nrm(ks[14], (DEPTH, 2, D_MODEL, D_FF), D_MODEL ** -0.5),
        'ffn_w_up': nrm(ks[15], (DEPTH, 2, D_MODEL, D_FF), D_MODEL ** -0.5),
        'ffn_w_down': nrm(ks[16], (DEPTH, 2, D_FF, D_MODEL), D_FF ** -0.5),
        'w_in': nrm(ks[17], (DEPTH, D_MODEL, D_IN), D_MODEL ** -0.5),
        'gqa_q_norm': 1.0 + nrm(ks[18], (DEPTH, HEAD_DIM), 0.05),
        'gqa_k_norm': 1.0 + nrm(ks[19], (DEPTH, HEAD_DIM), 0.05),
        'mlstm_gate_b': jnp.array([0.0, 3.0, 0.0, 3.0], jnp.float32)[None, :, None] + nrm(ks[20], (DEPTH, 4, MLSTM_HEADS), 0.1),
        'mlstm_out_norm': 1.0 + nrm(ks[21], (DEPTH, MLSTM_W), 0.05),
        'mla_q_norm': 1.0 + nrm(ks[22], (DEPTH, MLA_Q_RANK), 0.05),
        'mla_w_uq': nrm(ks[23], (DEPTH, MLA_Q_RANK, MLA_HEADS * MLA_QK), MLA_Q_RANK ** -0.5),
        'mla_kv_norm': 1.0 + nrm(ks[24], (DEPTH, MLA_KV_RANK), 0.05),
        'mla_w_ukv': nrm(ks[25], (DEPTH, MLA_KV_RANK, MLA_HEADS * (MLA_NOPE + MLA_V)), MLA_KV_RANK ** -0.5),
        'w_out': nrm(ks[26], (DEPTH, MIX_W, D_MODEL), MIX_W ** -0.5),
        'final_norm': 1.0 + nrm(ks[27], (D_MODEL,), 0.05),
    }


def reference(x_prompt, x_sample, c, cache_gqa_k, cache_gqa_v, cache_mla_ckv, cache_mla_krope,
              state_mlstm_C, state_mlstm_n, state_mlstm_m, c_ctx,
              w_ada, b_ada, norm_w, ffn_w_gate, ffn_w_up, ffn_w_down, w_in, gqa_q_norm, gqa_k_norm,
              mlstm_gate_b, mlstm_out_norm, mla_q_norm, mla_w_uq, mla_kv_norm, mla_w_ukv, w_out, final_norm):
    rows = x_sample.shape[1] // GRID_W
    rope_hd = axial_rope(rows, HEAD_DIM)
    rope_mla = axial_rope(rows, MLA_ROPE)
    cond_ctx = c_ctx[None, :]
    xp, xs = x_prompt, x_sample
    collected = [[] for _ in range(7)]
    for l in range(DEPTH):
        lp = {'w_ada': w_ada[l], 'b_ada': b_ada[l], 'norm_w': norm_w[l],
              'ffn_w_gate': ffn_w_gate[l], 'ffn_w_up': ffn_w_up[l], 'ffn_w_down': ffn_w_down[l],
              'w_in': w_in[l], 'gqa_q_norm': gqa_q_norm[l], 'gqa_k_norm': gqa_k_norm[l],
              'mlstm_gate_b': mlstm_gate_b[l], 'mlstm_out_norm': mlstm_out_norm[l],
              'mla_q_norm': mla_q_norm[l], 'mla_w_uq': mla_w_uq[l], 'mla_kv_norm': mla_kv_norm[l],
              'mla_w_ukv': mla_w_ukv[l], 'w_out': w_out[l]}
        xp, ctx_new = trunk_layer(xp, cond_ctx, lp, lambda h: mix_context(h, lp))
        for lst, t in zip(collected, ctx_new):
            lst.append(t)
        ctx_cached = (cache_gqa_k[:, l], cache_gqa_v[:, l], cache_mla_ckv[:, l], cache_mla_krope[:, l],
                      state_mlstm_C[:, l], state_mlstm_n[:, l], state_mlstm_m[:, l])
        xs, _ = trunk_layer(xs, c, lp, lambda h: (mix_latent(h, lp, ctx_cached, rope_hd, rope_mla), None))
    y_prompt = rms_norm(xp, final_norm)
    y_sample = rms_norm(xs, final_norm)
    new_gqa_k, new_gqa_v, new_mla_ckv, new_mla_krope, new_mlstm_C, new_mlstm_n, new_mlstm_m = [
        jnp.stack(lst, axis=1) for lst in collected]
    return (y_prompt, y_sample, new_gqa_k, new_gqa_v, new_mla_ckv, new_mla_krope, new_mlstm_C, new_mlstm_n, new_mlstm_m)
```

```python
import functools

import jax
import jax.numpy as jnp
from jax import lax
from jax.experimental import pallas as pl
from jax.experimental.pallas import tpu as pltpu

F32 = jnp.float32
BF16 = jnp.bfloat16

EPS = 1e-6
ROPE_BASE = 10000.0
GRID_W = 64
N_MOD = 9
HEAD_DIM = 64
GQA_HEADS = 6
GQA_KV_HEADS = 2
GQA_GROUP = GQA_HEADS // GQA_KV_HEADS
ML_HEADS = 4
ML_DK = 64
ML_W = ML_HEADS * ML_DK
MLA_HEADS = 6
MLA_RANK = 256
MLA_NOPE = 64
MLA_ROPE = 32
MLA_V = 64
MLA_QK = MLA_NOPE + MLA_ROPE
LANES = 128
COND_ROWS = 8
VMEM_LIMIT = 48 * 1024 * 1024

O_GQ = 0
O_GK = O_GQ + GQA_HEADS * LANES
O_GV = O_GK + GQA_KV_HEADS * LANES
O_MQ = O_GV + GQA_KV_HEADS * HEAD_DIM
O_MK = O_MQ + ML_W
O_MV = O_MK + ML_W
O_MO = O_MV + ML_W
O_QL = O_MO + ML_W
O_KV = O_QL + MLA_RANK
O_KR = O_KV + MLA_RANK
O_MG = O_KR + LANES
D_INP = O_MG + LANES


def _params(sem):
    return pltpu.CompilerParams(dimension_semantics=sem, vmem_limit_bytes=VMEM_LIMIT)


def _norm_mod(x, nw, sc, sh):
    ms = jnp.mean(x * x, axis=-1, keepdims=True)
    return (x * lax.rsqrt(ms + EPS) * nw) * (1.0 + sc) + sh


def _dot(a, b):
    return jnp.dot(a, b, preferred_element_type=F32)


def _dot_nt(a, b):
    return lax.dot_general(a, b, (((1,), (1,)), ((), ())), preferred_element_type=F32)


def _dot_tn(a, b):
    return lax.dot_general(a, b, (((0,), (0,)), ((), ())), preferred_element_type=F32)


def _split3(x):
    hi = x.astype(BF16)
    r1 = x - hi.astype(F32)
    mid = r1.astype(BF16)
    lo = (r1 - mid.astype(F32)).astype(BF16)
    return hi, mid, lo


def _dot_exact_lhs(a_bf16, x):
    hi, mid, lo = _split3(x)
    return _dot(a_bf16, hi) + _dot(a_bf16, mid) + _dot(a_bf16, lo)


def _ada_kernel(cond_ref, w_ref, b_ref, o_ref):
    c = cond_ref[...]
    s = (c * jax.nn.sigmoid(c)).astype(BF16)
    o_ref[...] = _dot(s, w_ref[...].astype(BF16)) + b_ref[...]


def _ada(cond, w_ada, b_ada):
    depth, d, nd = w_ada.shape
    tn = d
    return pl.pallas_call(
        _ada_kernel,
        grid=(depth, nd // tn),
        in_specs=[
            pl.BlockSpec((COND_ROWS, d), lambda l, j: (0, 0)),
            pl.BlockSpec((None, d, tn), lambda l, j: (l, 0, j)),
            pl.BlockSpec((None, 1, tn), lambda l, j: (l, 0, j)),
        ],
        out_specs=pl.BlockSpec((None, COND_ROWS, tn), lambda l, j: (l, 0, j)),
        out_shape=jax.ShapeDtypeStruct((depth, COND_ROWS, nd), F32),
        compiler_params=_params(("parallel", "parallel")),
        name="ada",
    )(cond, w_ada, b_ada.reshape(depth, 1, nd))


def _ffn_kernel(x_ref, mod_ref, nw_ref, wg_ref, wu_ref, wd_ref, o_ref, h_ref, acc_ref, *, mi, nf):
    f = pl.program_id(1)

    @pl.when(f == 0)
    def _():
        h = _norm_mod(x_ref[...], nw_ref[...], mod_ref[mi + 1:mi + 2, :], mod_ref[mi:mi + 1, :])
        h_ref[...] = h.astype(BF16)
        acc_ref[...] = jnp.zeros_like(acc_ref)

    h = h_ref[...]
    g = _dot(h, wg_ref[...])
    u = _dot(h, wu_ref[...])
    a = (g * jax.nn.sigmoid(g) * u).astype(BF16)
    acc_ref[...] += _dot(a, wd_ref[...])

    @pl.when(f == nf - 1)
    def _():
        o_ref[...] = x_ref[...] + (0.5 * mod_ref[mi + 2:mi + 3, :]) * acc_ref[...]


def _ffn(x, mods, norm_w4, wg, wu, wd, l, j, group_of_tile, tm, tf):
    n, d = x.shape
    dff = wg.shape[-1]
    nf = dff // tf
    mi = 0 if j == 0 else 6
    nwi = 0 if j == 0 else 2
    return pl.pallas_call(
        functools.partial(_ffn_kernel, mi=mi, nf=nf),
        grid=(n // tm, nf),
        in_specs=[
            pl.BlockSpec((tm, d), lambda i, f: (i, 0)),
            pl.BlockSpec((None, None, N_MOD, d), lambda i, f: (l, group_of_tile(i), 0, 0)),
            pl.BlockSpec((None, None, 1, d), lambda i, f: (l, nwi, 0, 0)),
            pl.BlockSpec((None, None, d, tf), lambda i, f: (l, j, 0, f)),
            pl.BlockSpec((None, None, d, tf), lambda i, f: (l, j, 0, f)),
            pl.BlockSpec((None, None, tf, d), lambda i, f: (l, j, f, 0)),
        ],
        out_specs=pl.BlockSpec((tm, d), lambda i, f: (i, 0)),
        out_shape=jax.ShapeDtypeStruct((n, d), F32),
        scratch_shapes=[pltpu.VMEM((tm, d), BF16), pltpu.VMEM((tm, d), F32)],
        compiler_params=_params(("parallel", "arbitrary")),
        name="ffn",
    )(x, mods, norm_w4, wg, wu, wd)


def _inproj_kernel(*refs, rope, ctx_out):
    it = iter(refs)
    x_ref, mod_ref, nw_ref, w_ref, qn_ref, kn_ref, gb_ref, mqn_ref, mkvn_ref, wq_ref, wk_ref, wv_ref = (
        next(it) for _ in range(12))
    if rope:
        cq_ref, sq_ref, cm_ref, sm_ref = (next(it) for _ in range(4))
    gq_o, gk_o, gv_o, mq_o, mk_o, mv_o, mo_o, mg_o, qm_o, kp_o, vm_o = (next(it) for _ in range(11))
    if ctx_out:
        gkf_o, gvf_o, ckv_o, kr_o = (next(it) for _ in range(4))

    x = x_ref[...]
    tm = x.shape[0]
    hb = _norm_mod(x, nw_ref[...], mod_ref[4:5, :], mod_ref[3:4, :]).astype(BF16)

    def proj(o, n):
        return _dot(hb, w_ref[:, o:o + n])

    lane = lax.broadcasted_iota(jnp.int32, (tm, LANES), 1)

    def head_norm(xg, wrow):
        ms = jnp.sum(xg * xg, axis=-1, keepdims=True) * (1.0 / HEAD_DIM)
        return xg * lax.rsqrt(ms + EPS) * wrow

    def rope_hd(xg):
        half = HEAD_DIM // 2
        partner = jnp.where(lane < half, pltpu.roll(xg, LANES - half, 1), pltpu.roll(xg, half, 1))
        return xg * cq_ref[...] + partner * sq_ref[...]

    def rope_mla(xg):
        half = MLA_ROPE // 2
        partner = jnp.where((lane & half) == 0, pltpu.roll(xg, LANES - half, 1), pltpu.roll(xg, half, 1))
        return xg * cm_ref[...] + partner * sm_ref[...]

    gq = proj(O_GQ, GQA_HEADS * LANES)
    for g in range(GQA_HEADS):
        qg = head_norm(gq[:, g * LANES:(g + 1) * LANES], qn_ref[...])
        if rope:
            qg = rope_hd(qg)
        gq_o[:, g * LANES:(g + 1) * LANES] = (qg * (HEAD_DIM ** -0.5)).astype(BF16)
    gk = proj(O_GK, GQA_KV_HEADS * LANES)
    for g in range(GQA_KV_HEADS):
        kg = head_norm(gk[:, g * LANES:(g + 1) * LANES], kn_ref[...])
        if ctx_out:
            gkf_o[:, g * LANES:(g + 1) * LANES] = kg
        if rope:
            kg = rope_hd(kg)
        gk_o[:, g * LANES:(g + 1) * LANES] = kg.astype(BF16)
    gv = proj(O_GV, GQA_KV_HEADS * HEAD_DIM)
    gv_o[...] = gv.astype(BF16)
    if ctx_out:
        gvf_o[...] = gv

    mq_o[...] = (proj(O_MQ, ML_W) * (ML_DK ** -0.5)).astype(BF16)
    mk_o[...] = proj(O_MK, ML_W).astype(BF16)
    mv_o[...] = proj(O_MV, ML_W).astype(BF16)
    mo_o[...] = proj(O_MO, ML_W)
    mg_o[...] = proj(O_MG, LANES) + gb_ref[...]

    xq = proj(O_QL, MLA_RANK)
    ms = jnp.mean(xq * xq, axis=-1, keepdims=True)
    qn = (xq * lax.rsqrt(ms + EPS) * mqn_ref[...]).astype(BF16)
    qm = _dot(qn, wq_ref[...])
    for g in range(MLA_HEADS):
        qg = qm[:, g * LANES:(g + 1) * LANES]
        if rope:
            qg = rope_mla(qg)
        qm_o[:, g * LANES:(g + 1) * LANES] = (qg * (MLA_QK ** -0.5)).astype(BF16)

    xkv = proj(O_KV, MLA_RANK)
    ms = jnp.mean(xkv * xkv, axis=-1, keepdims=True)
    ckv = xkv * lax.rsqrt(ms + EPS) * mkvn_ref[...]
    kr = proj(O_KR, LANES)
    if rope:
        kr = rope_mla(kr)
    if ctx_out:
        ckv_o[...] = ckv
        kr_o[...] = kr
    cb = ckv.astype(BF16)
    kk = _dot(cb, wk_ref[...])
    for g in range(MLA_HEADS):
        kp_o[:, g * LANES:(g + 1) * LANES] = (kk[:, g * LANES:(g + 1) * LANES] + kr).astype(BF16)
    vm_o[...] = _dot(cb, wv_ref[...]).astype(BF16)


def _inproj(x, mods, norm_w4, w_in_p, qn, kn, gb, mqn, mkvn, wq, wk, wv, rope_tabs, l, group_of_tile, tm,
            rows_per_seq, ctx_out):
    n, d = x.shape
    rope = rope_tabs is not None
    row = lambda i: (i, 0)
    lay = lambda shape: pl.BlockSpec((None,) + shape, lambda i: (l,) + (0,) * len(shape))
    in_specs = [
        pl.BlockSpec((tm, d), row),
        pl.BlockSpec((None, None, N_MOD, d), lambda i: (l, group_of_tile(i), 0, 0)),
        pl.BlockSpec((None, None, 1, d), lambda i: (l, 1, 0, 0)),
        lay((d, D_INP)), lay((1, LANES)), lay((1, LANES)), lay((1, LANES)), lay((1, MLA_RANK)), lay((1, MLA_RANK)),
        lay((MLA_RANK, MLA_HEADS * LANES)), lay((MLA_RANK, MLA_HEADS * LANES)), lay((MLA_RANK, MLA_HEADS * MLA_V)),
    ]
    args = [x, mods, norm_w4, w_in_p, qn, kn, gb, mqn, mkvn, wq, wk, wv]
    if rope:
        tiles_per_seq = rows_per_seq // tm
        tab = pl.BlockSpec((tm, LANES), lambda i: (i % tiles_per_seq, 0))
        in_specs += [tab] * 4
        args += list(rope_tabs)
    widths = [(GQA_HEADS * LANES, BF16), (GQA_KV_HEADS * LANES, BF16), (GQA_KV_HEADS * HEAD_DIM, BF16),
              (ML_W, BF16), (ML_W, BF16), (ML_W, BF16), (ML_W, F32), (LANES, F32),
              (MLA_HEADS * LANES, BF16), (MLA_HEADS * LANES, BF16), (MLA_HEADS * MLA_V, BF16)]
    if ctx_out:
        widths += [(GQA_KV_HEADS * LANES, F32), (GQA_KV_HEADS * HEAD_DIM, F32), (MLA_RANK, F32), (LANES, F32)]
    return pl.pallas_call(
        functools.partial(_inproj_kernel, rope=rope, ctx_out=ctx_out),
        grid=(n // tm,),
        in_specs=in_specs,
        out_specs=[pl.BlockSpec((tm, w), row) for w, _ in widths],
        out_shape=[jax.ShapeDtypeStruct((n, w), dt) for w, dt in widths],
        compiler_params=_params(("parallel",)),
        name="inproj",
    )(*args)


def _kvexp_kernel(ckv_ref, kr_ref, wk_ref, wv_ref, kp_o, vm_o):
    cb = ckv_ref[...].astype(BF16)
    kk = _dot(cb, wk_ref[...])
    kr = kr_ref[...]
    for g in range(MLA_HEADS):
        kp_o[:, g * LANES:(g + 1) * LANES] = (kk[:, g * LANES:(g + 1) * LANES] + kr).astype(BF16)
    vm_o[...] = _dot(cb, wv_ref[...]).astype(BF16)


def _kvexp(ckv, kr_slot, wk, wv, l):
    n = ckv.shape[0]
    lay = lambda shape: pl.BlockSpec((None,) + shape, lambda i: (l,) + (0,) * len(shape))
    return pl.pallas_call(
        _kvexp_kernel,
        grid=(1,),
        in_specs=[pl.BlockSpec((n, MLA_RANK), lambda i: (0, 0)), pl.BlockSpec((n, LANES), lambda i: (0, 0)),
                  lay((MLA_RANK, MLA_HEADS * LANES)), lay((MLA_RANK, MLA_HEADS * MLA_V))],
        out_specs=[pl.BlockSpec((n, MLA_HEADS * LANES), lambda i: (0, 0)),
                   pl.BlockSpec((n, MLA_HEADS * MLA_V), lambda i: (0, 0))],
        out_shape=[jax.ShapeDtypeStruct((n, MLA_HEADS * LANES), BF16),
                   jax.ShapeDtypeStruct((n, MLA_HEADS * MLA_V), BF16)],
        compiler_params=_params(("arbitrary",)),
        name="kvexp",
    )(ckv, kr_slot, wk, wv)


def _attn_kernel(*refs, n_src, heads, group):
    q_ref = refs[0]
    k_refs = refs[1:1 + n_src]
    v_refs = refs[1 + n_src:1 + 2 * n_src]
    o_ref = refs[1 + 2 * n_src]
    outs = []
    for h in range(heads):
        kvh = h // group
        qh = q_ref[:, h * LANES:(h + 1) * LANES]
        ss = [_dot_nt(qh, k[:, kvh * LANES:(kvh + 1) * LANES]) for k in k_refs]
        m = jnp.max(ss[0], axis=-1, keepdims=True)
        for s in ss[1:]:
            m = jnp.maximum(m, jnp.max(s, axis=-1, keepdims=True))
        vb, sub = kvh // 2, (kvh % 2) * MLA_V
        den = None
        acc = None
        for s, v in zip(ss, v_refs):
            p = jnp.exp(s - m)
            ps = jnp.sum(p, axis=-1, keepdims=True)
            pv = _dot(p.astype(BF16), v[:, vb * LANES:(vb + 1) * LANES])
            den = ps if den is None else den + ps
            acc = pv if acc is None else acc + pv
        outs.append(acc[:, sub:sub + MLA_V] / den)
    o_ref[...] = jnp.concatenate(outs, axis=-1).astype(BF16)


def _attn(q, ks, vs, batch, heads, group, tq):
    n = q.shape[0]
    t = n // batch
    nq = t // tq
    n_src = len(ks)
    in_specs = [pl.BlockSpec((tq, heads * LANES), lambda b, i: (b * nq + i, 0))]
    for a in list(ks) + list(vs):
        s = a.shape[0] // batch
        in_specs.append(pl.BlockSpec((s, a.shape[1]), lambda b, i: (b, 0)))
    return pl.pallas_call(
        functools.partial(_attn_kernel, n_src=n_src, heads=heads, group=group),
        grid=(batch, nq),
        in_specs=in_specs,
        out_specs=pl.BlockSpec((tq, heads * MLA_V), lambda b, i: (b * nq + i, 0)),
        out_shape=jax.ShapeDtypeStruct((n, heads * MLA_V), BF16),
        compiler_params=_params(("parallel", "parallel")),
        name="attn",
    )(q, *ks, *vs)


def _mlstm_kernel(*refs, chunk, nc, has_init, want_state):
    it = iter(refs)
    fwd = tuple(next(it) for _ in range(4))
    bwd = tuple(next(it) for _ in range(4))
    if has_init:
        c0_ref, n0_ref, m0_ref = (next(it) for _ in range(3))
    h_outs = (next(it), next(it))
    if want_state:
        c_out, n_out, m_out = (next(it) for _ in range(3))
    c_s, n_s, m_s = (next(it) for _ in range(3))
    step = pl.program_id(1)

    @pl.when(step == 0)
    def _():
        if has_init:
            c_s[...] = c0_ref[...]
            n_s[...] = n0_ref[...]
            m_s[...] = m0_ref[...]
        else:
            c_s[...] = jnp.zeros_like(c_s)
            n_s[...] = jnp.zeros_like(n_s)
            m_s[...] = jnp.zeros_like(m_s)

    row = lax.broadcasted_iota(jnp.int32, (chunk, chunk), 0)
    col = lax.broadcasted_iota(jnp.int32, (chunk, chunk), 1)
    for d in range(2):
        q_ref, k_ref, v_ref, g_ref = fwd if d == 0 else bwd
        valid = (col <= row) if d == 0 else (col >= row)
        gates = g_ref[...]
        logf = jax.nn.log_sigmoid(gates)
        bcum = _dot_exact_lhs(valid.astype(BF16), logf)
        gates_t = gates.T
        bcum_t = bcum.T
        hs = []
        for h in range(ML_HEADS):
            ci = (2 * d) * ML_HEADS + h
            cf = (2 * d + 1) * ML_HEADS + h
            li_c, b_c = gates[:, ci:ci + 1], bcum[:, cf:cf + 1]
            li_r, b_r = gates_t[ci:ci + 1, :], bcum_t[cf:cf + 1, :]
            qh = q_ref[:, h * ML_DK:(h + 1) * ML_DK]
            kh = k_ref[:, h * ML_DK:(h + 1) * ML_DK]
            vh = v_ref[:, h * ML_DK:(h + 1) * ML_DK]
            c_prev = c_s[d, h]
            n_prev = n_s[d, h:h + 1, :]
            m_prev = m_s[d:d + 1, h:h + 1]

            dmat = jnp.where(valid, b_c - b_r + li_r, -jnp.inf)
            inter = b_c + m_prev
            m_t = jnp.maximum(inter, jnp.max(dmat, axis=-1, keepdims=True))
            w_inter = jnp.exp(inter - m_t)
            qk = _dot_nt(qh, kh) * jnp.exp(dmat - m_t)
            num = w_inter * _dot(qh, c_prev.astype(BF16)) + _dot(qk.astype(BF16), vh)
            den = (w_inter * jnp.sum(qh.astype(F32) * n_prev, axis=-1, keepdims=True)
                   + jnp.sum(qk, axis=-1, keepdims=True))
            hs.append(num / jnp.maximum(jnp.abs(den), jnp.exp(-m_t)))

            b_last = b_r[:, chunk - 1:chunk] if d == 0 else b_r[:, 0:1]
            m_new = jnp.maximum(b_last + m_prev, jnp.max(b_last - b_r + li_r, axis=-1, keepdims=True))
            w_old = jnp.exp(b_last + m_prev - m_new)
            kw = kh.astype(F32) * jnp.exp(b_last - b_c + li_c - m_new)
            c_s[d, h] = w_old * c_prev + _dot_tn(kw.astype(BF16), vh)
            n_s[d, h:h + 1, :] = w_old * n_prev + jnp.sum(kw, axis=0, keepdims=True)
            m_s[d:d + 1, h:h + 1] = m_new
        h_outs[d][...] = jnp.concatenate(hs, axis=-1)

    if want_state:
        @pl.when(step == nc - 1)
        def _():
            c_out[...] = c_s[...]
            n_out[...] = n_s[...]
            m_out[...] = m_s[...]


def _mlstm(mq, mk, mv, mg, batch, chunk, init, l, want_state):
    n = mq.shape[0]
    t = n // batch
    nc = t // chunk
    fwd_map = lambda b, c: (b * nc + c, 0)
    bwd_map = lambda b, c: (b * nc + nc - 1 - c, 0)
    in_specs, args = [], []
    for imap in (fwd_map, bwd_map):
        in_specs += [pl.BlockSpec((chunk, ML_W), imap)] * 3 + [pl.BlockSpec((chunk, LANES), imap)]
        args += [mq, mk, mv, mg]
    has_init = init is not None
    if has_init:
        in_specs += [
            pl.BlockSpec((None, None, 2, ML_HEADS, ML_DK, ML_DK), lambda b, c: (b, l, 0, 0, 0, 0)),
            pl.BlockSpec((None, None, 2, ML_HEADS, ML_DK), lambda b, c: (b, l, 0, 0, 0)),
            pl.BlockSpec((None, None, 2, ML_HEADS), lambda b, c: (b, l, 0, 0)),
        ]
        args += list(init)
    out_specs = [pl.BlockSpec((chunk, ML_W), fwd_map), pl.BlockSpec((chunk, ML_W), bwd_map)]
    out_shape = [jax.ShapeDtypeStruct((n, ML_W), F32)] * 2
    if want_state:
        out_specs += [
            pl.BlockSpec((None, 2, ML_HEADS, ML_DK, ML_DK), lambda b, c: (b, 0, 0, 0, 0)),
            pl.BlockSpec((None, 2, ML_HEADS, ML_DK), lambda b, c: (b, 0, 0, 0)),
            pl.BlockSpec((None, 2, ML_HEADS), lambda b, c: (b, 0, 0)),
        ]
        out_shape += [jax.ShapeDtypeStruct((batch, 2, ML_HEADS, ML_DK, ML_DK), F32),
                      jax.ShapeDtypeStruct((batch, 2, ML_HEADS, ML_DK), F32),
                      jax.ShapeDtypeStruct((batch, 2, ML_HEADS), F32)]
    return pl.pallas_call(
        functools.partial(_mlstm_kernel, chunk=chunk, nc=nc, has_init=has_init, want_state=want_state),
        grid=(batch, nc),
        in_specs=in_specs,
        out_specs=out_specs,
        out_shape=out_shape,
        scratch_shapes=[pltpu.VMEM((2, ML_HEADS, ML_DK, ML_DK), F32), pltpu.VMEM((2, ML_HEADS, ML_DK), F32),
                        pltpu.VMEM((2, ML_HEADS), F32)],
        compiler_params=_params(("parallel", "arbitrary")),
        name="mlstm",
    )(*args)


def _outproj_kernel(x_ref, mod_ref, go_ref, hf_ref, hb_ref, mo_ref, lo_ref, onw_ref, seg_ref, w_ref, o_ref):
    gqa_w = GQA_HEADS * HEAD_DIM
    hsum = hf_ref[...] + hb_ref[...]
    hi, mid, lo = _split3(hsum * hsum)
    seg = seg_ref[...]
    ms = _dot(hi, seg) + _dot(mid, seg) + _dot(lo, seg)
    hn = hsum * lax.rsqrt(ms + EPS) * onw_ref[...]
    ml = (hn * jax.nn.sigmoid(mo_ref[...])).astype(BF16)
    out = (_dot(go_ref[...], w_ref[0:gqa_w, :]) + _dot(ml, w_ref[gqa_w:gqa_w + ML_W, :])
           + _dot(lo_ref[...], w_ref[gqa_w + ML_W:, :]))
    o_ref[...] = x_ref[...] + mod_ref[5:6, :] * out


def _outproj(x, mods, gqa_o, hf, hb, mo, mla_o, onw, seg, w_out, l, group_of_tile, tm):
    n, d = x.shape
    row = lambda i: (i, 0)
    lay = lambda shape: pl.BlockSpec((None,) + shape, lambda i: (l,) + (0,) * len(shape))
    return pl.pallas_call(
        _outproj_kernel,
        grid=(n // tm,),
        in_specs=[
            pl.BlockSpec((tm, d), row),
            pl.BlockSpec((None, None, N_MOD, d), lambda i: (l, group_of_tile(i), 0, 0)),
            pl.BlockSpec((tm, GQA_HEADS * HEAD_DIM), row),
            pl.BlockSpec((tm, ML_W), row), pl.BlockSpec((tm, ML_W), row), pl.BlockSpec((tm, ML_W), row),
            pl.BlockSpec((tm, MLA_HEADS * MLA_V), row),
            lay((1, ML_W)),
            pl.BlockSpec((ML_W, ML_W), lambda i: (0, 0)),
            lay((w_out.shape[1], d)),
        ],
        out_specs=pl.BlockSpec((tm, d), row),
        out_shape=jax.ShapeDtypeStruct((n, d), F32),
        compiler_params=_params(("parallel",)),
        name="outproj",
    )(x, mods, gqa_o, hf, hb, mo, mla_o, onw, seg, w_out)


def _final_norm_kernel(x_ref, w_ref, o_ref):
    x = x_ref[...]
    ms = jnp.mean(x * x, axis=-1, keepdims=True)
    o_ref[...] = x * lax.rsqrt(ms + EPS) * w_ref[...]


def _final_norm(x, w, tm):
    n, d = x.shape
    return pl.pallas_call(
        _final_norm_kernel,
        grid=(n // tm,),
        in_specs=[pl.BlockSpec((tm, d), lambda i: (i, 0)), pl.BlockSpec((1, d), lambda i: (0, 0))],
        out_specs=pl.BlockSpec((tm, d), lambda i: (i, 0)),
        out_shape=jax.ShapeDtypeStruct((n, d), F32),
        compiler_params=_params(("parallel",)),
        name="final_norm",
    )(x, w.reshape(1, d))


def _pad_last(a, width):
    return jnp.pad(a, [(0, 0)] * (a.ndim - 1) + [(0, width - a.shape[-1])])


def _pad_heads(a, heads, width):
    lead = a.shape[:-1]
    return _pad_last(a.reshape(lead + (heads, width)), LANES).reshape(lead + (heads * LANES,))


def _axial_rope(seq, rot_dim):
    half = rot_dim // 2
    rows = seq // GRID_W
    freqs = ROPE_BASE ** (-jnp.arange(0, half, 2, dtype=F32) / half)
    r = jnp.repeat(jnp.arange(rows, dtype=F32), GRID_W)
    c = jnp.tile(jnp.arange(GRID_W, dtype=F32), rows)
    ang = jnp.concatenate([r[:, None] * freqs, c[:, None] * freqs], axis=-1)
    return jnp.cos(ang), jnp.sin(ang)


def _rope_tables(seq):
    c, s = _axial_rope(seq, HEAD_DIM)
    ones = jnp.ones((seq, LANES - HEAD_DIM), F32)
    cq = jnp.concatenate([c, c, ones], axis=-1)
    sq = jnp.concatenate([-s, s, 0.0 * ones], axis=-1)
    c, s = _axial_rope(seq, MLA_ROPE)
    one_a = jnp.ones((seq, MLA_NOPE), F32)
    one_b = jnp.ones((seq, LANES - MLA_QK), F32)
    cm = jnp.concatenate([one_a, c, c, one_b], axis=-1)
    sm = jnp.concatenate([0.0 * one_a, -s, s, 0.0 * one_b], axis=-1)
    return cq, sq, cm, sm


def _pick_tile(n, pref):
    t = min(n, pref)
    while n % t:
        t //= 2
    return t


def kernel(x_prompt, x_sample, c, cache_gqa_k, cache_gqa_v, cache_mla_ckv, cache_mla_krope, state_mlstm_C, state_mlstm_n, state_mlstm_m, c_ctx, w_ada, b_ada, norm_w, ffn_w_gate, ffn_w_up, ffn_w_down, w_in, gqa_q_norm, gqa_k_norm, mlstm_gate_b, mlstm_out_norm, mla_q_norm, mla_w_uq, mla_kv_norm, mla_w_ukv, w_out, final_norm):
    bc, tc, d = x_prompt.shape
    bl, tl, _ = x_sample.shape
    depth = w_ada.shape[0]
    past = cache_gqa_k.shape[2]
    assert 1 + bl <= COND_ROWS

    cond = jnp.concatenate([c_ctx[None, :], c, jnp.zeros((COND_ROWS - 1 - bl, d), F32)], axis=0)
    mods = _ada(cond, w_ada, b_ada).reshape(depth, COND_ROWS, N_MOD, d)

    wg, wu, wd = ffn_w_gate.astype(BF16), ffn_w_up.astype(BF16), ffn_w_down.astype(BF16)
    idx = [0]
    for wdt in (GQA_HEADS * HEAD_DIM, GQA_KV_HEADS * HEAD_DIM, GQA_KV_HEADS * HEAD_DIM, ML_W, ML_W, ML_W, ML_W,
                4 * ML_HEADS, MLA_RANK, MLA_RANK, MLA_ROPE):
        idx.append(idx[-1] + wdt)
    seg = [w_in[..., a:b] for a, b in zip(idx[:-1], idx[1:])]
    s_gq, s_gk, s_gv, s_mq, s_mk, s_mv, s_mo, s_mg, s_ql, s_kv, s_kr = seg
    kr_slot_w = jnp.pad(s_kr, [(0, 0), (0, 0), (MLA_NOPE, LANES - MLA_QK)])
    w_in_p = jnp.concatenate(
        [_pad_heads(s_gq, GQA_HEADS, HEAD_DIM), _pad_heads(s_gk, GQA_KV_HEADS, HEAD_DIM), s_gv,
         s_mq, s_mk, s_mv, s_mo, s_ql, s_kv, kr_slot_w, _pad_last(s_mg, LANES)], axis=-1).astype(BF16)
    assert w_in_p.shape[-1] == D_INP
    wq = _pad_heads(mla_w_uq, MLA_HEADS, MLA_QK).astype(BF16)
    ukv = mla_w_ukv.reshape(depth, MLA_RANK, MLA_HEADS, MLA_NOPE + MLA_V)
    wk = _pad_last(ukv[..., :MLA_NOPE], LANES).reshape(depth, MLA_RANK, MLA_HEADS * LANES).astype(BF16)
    wv = ukv[..., MLA_NOPE:].reshape(depth, MLA_RANK, MLA_HEADS * MLA_V).astype(BF16)
    w_out_b = w_out.astype(BF16)
    qn = _pad_last(gqa_q_norm, LANES).reshape(depth, 1, LANES)
    kn = _pad_last(gqa_k_norm, LANES).reshape(depth, 1, LANES)
    gb = _pad_last(mlstm_gate_b.reshape(depth, 4 * ML_HEADS), LANES).reshape(depth, 1, LANES)
    mqn = mla_q_norm.reshape(depth, 1, MLA_RANK)
    mkvn = mla_kv_norm.reshape(depth, 1, MLA_RANK)
    onw = mlstm_out_norm.reshape(depth, 1, ML_W)
    norm_w4 = norm_w.reshape(depth, 3, 1, d)
    head_id = jnp.arange(ML_W) // ML_DK
    seg_mean = ((head_id[:, None] == head_id[None, :]).astype(F32) / ML_DK).astype(BF16)
    rope_tabs = _rope_tables(tl)

    ck_gqa = _pad_heads(cache_gqa_k.reshape(bl, depth, past, GQA_KV_HEADS * HEAD_DIM), GQA_KV_HEADS,
                        HEAD_DIM).astype(BF16)
    cv_gqa = cache_gqa_v.reshape(bl, depth, past, GQA_KV_HEADS * HEAD_DIM).astype(BF16)
    ckr_slot = jnp.pad(cache_mla_krope, [(0, 0), (0, 0), (0, 0), (MLA_NOPE, LANES - MLA_QK)])

    nc_rows, nl_rows = bc * tc, bl * tl
    tm_c, tm_l = _pick_tile(nc_rows, 512), _pick_tile(tl, 512)
    tf = 256
    tq_c, tq_l = _pick_tile(tc, 256), _pick_tile(tl, 256)
    chunk_c, chunk_l = _pick_tile(tc, 128), _pick_tile(tl, 128)
    ctx_group = lambda i: 0
    lat_group = lambda i: 1 + i // (tl // tm_l)

    xc = x_prompt.reshape(nc_rows, d)
    xl = x_sample.reshape(nl_rows, d)
    collected = [[] for _ in range(7)]
    for l in range(depth):
        xc = _ffn(xc, mods, norm_w4, wg, wu, wd, l, 0, ctx_group, tm_c, tf)
        xl = _ffn(xl, mods, norm_w4, wg, wu, wd, l, 0, lat_group, tm_l, tf)

        (gq, gk, gv, mq, mk, mv, mo, mg, qm, kp, vm, gkf, gvf, ckv, krs) = _inproj(
            xc, mods, norm_w4, w_in_p, qn, kn, gb, mqn, mkvn, wq, wk, wv, None, l, ctx_group, tm_c, tc, True)
        gqa_o = _attn(gq, [gk], [gv], bc, GQA_HEADS, GQA_GROUP, tq_c)
        mla_o = _attn(qm, [kp], [vm], bc, MLA_HEADS, 1, tq_c)
        hf, hb, c_new, n_new, m_new = _mlstm(mq, mk, mv, mg, bc, chunk_c, None, l, True)
        xc = _outproj(xc, mods, gqa_o, hf, hb, mo, mla_o, onw, seg_mean, w_out_b, l, ctx_group, tm_c)
        new_k = gkf.reshape(bc, tc, GQA_KV_HEADS, LANES)[..., :HEAD_DIM]
        new_v = gvf.reshape(bc, tc, GQA_KV_HEADS, HEAD_DIM)
        new_ckv = ckv.reshape(bc, tc, MLA_RANK)
        new_kr = krs.reshape(bc, tc, LANES)[..., MLA_NOPE:MLA_QK]
        for lst, t in zip(collected, (new_k, new_v, new_ckv, new_kr, c_new, n_new, m_new)):
            lst.append(t)

        (gq, gk, gv, mq, mk, mv, mo, mg, qm, kp, vm) = _inproj(
            xl, mods, norm_w4, w_in_p, qn, kn, gb, mqn, mkvn, wq, wk, wv, rope_tabs, l, lat_group, tm_l, tl, False)
        kp_c, vm_c = _kvexp(cache_mla_ckv[:, l].reshape(bl * past, MLA_RANK), ckr_slot[:, l].reshape(bl * past, LANES),
                            wk, wv, l)
        gqa_o = _attn(gq, [ck_gqa[:, l].reshape(bl * past, -1), gk], [cv_gqa[:, l].reshape(bl * past, -1), gv],
                      bl, GQA_HEADS, GQA_GROUP, tq_l)
        mla_o = _attn(qm, [kp_c, kp], [vm_c, vm], bl, MLA_HEADS, 1, tq_l)
        hf, hb = _mlstm(mq, mk, mv, mg, bl, chunk_l, (state_mlstm_C, state_mlstm_n, state_mlstm_m), l, False)
        xl = _outproj(xl, mods, gqa_o, hf, hb, mo, mla_o, onw, seg_mean, w_out_b, l, lat_group, tm_l)

        xc = _ffn(xc, mods, norm_w4, wg, wu, wd, l, 1, ctx_group, tm_c, tf)
        xl = _ffn(xl, mods, norm_w4, wg, wu, wd, l, 1, lat_group, tm_l, tf)

    y_prompt = _final_norm(xc, final_norm, tm_c).reshape(bc, tc, d)
    y_sample = _final_norm(xl, final_norm, tm_l).reshape(bl, tl, d)
    stacked = [jnp.stack(lst, axis=1) for lst in collected]
    return (y_prompt, y_sample, *stacked)
```

```python
import functools

import jax
import jax.numpy as jnp
from jax import lax
from jax.experimental import pallas as pl
from jax.experimental.pallas import tpu as pltpu

F32 = jnp.float32
BF16 = jnp.bfloat16

EPS = 1e-6
ROPE_BASE = 10000.0
GRID_W = 64
N_MOD = 9
HEAD_DIM = 64
GQA_HEADS = 6
GQA_KV_HEADS = 2
GQA_GROUP = GQA_HEADS // GQA_KV_HEADS
ML_HEADS = 4
ML_DK = 64
ML_W = ML_HEADS * ML_DK
MLA_HEADS = 6
MLA_RANK = 256
MLA_NOPE = 64
MLA_ROPE = 32
MLA_V = 64
MLA_QK = MLA_NOPE + MLA_ROPE
LANES = 128
COND_ROWS = 8
VMEM_LIMIT = 48 * 1024 * 1024
LOG2E = 1.4426950408889634
REDUCE_ROWS = 64

O_GQ = 0
O_GK = O_GQ + GQA_HEADS * LANES
O_GV = O_GK + GQA_KV_HEADS * LANES
O_MK = O_GV + GQA_KV_HEADS * HEAD_DIM
O_MO = O_MK + ML_W
O_QL = O_MO + ML_W
O_KV = O_QL + MLA_RANK
O_KR = O_KV + MLA_RANK
O_GI = O_KR + LANES
O_GF = O_GI + LANES
D_INP = O_GF + LANES


def _params(sem):
    return pltpu.CompilerParams(dimension_semantics=sem, vmem_limit_bytes=VMEM_LIMIT)


def _norm_mod(x, nw, sc, sh):
    ms = jnp.mean(x * x, axis=-1, keepdims=True)
    return (x * lax.rsqrt(ms + EPS) * nw) * (1.0 + sc) + sh


def _dot(a, b):
    return jnp.dot(a, b, preferred_element_type=F32)


def _dot_nt(a, b):
    return lax.dot_general(a, b, (((1,), (1,)), ((), ())), preferred_element_type=F32)


def _dot_tn(a, b):
    return lax.dot_general(a, b, (((0,), (0,)), ((), ())), preferred_element_type=F32)


def _split3(x):
    hi = x.astype(BF16)
    r1 = x - hi.astype(F32)
    mid = r1.astype(BF16)
    lo = (r1 - mid.astype(F32)).astype(BF16)
    return hi, mid, lo


def _dot_exact_lhs(a_bf16, x):
    hi, mid, lo = _split3(x)
    return _dot(a_bf16, hi) + _dot(a_bf16, mid) + _dot(a_bf16, lo)


def _ada_kernel(cond_ref, w_ref, b_ref, o_ref):
    c = cond_ref[...]
    s = (c * jax.nn.sigmoid(c)).astype(BF16)
    o_ref[...] = _dot(s, w_ref[...].astype(BF16)) + b_ref[...]


def _ada(cond, w_ada, b_ada):
    depth, d, nd = w_ada.shape
    tn = d
    return pl.pallas_call(
        _ada_kernel,
        grid=(depth, nd // tn),
        in_specs=[
            pl.BlockSpec((COND_ROWS, d), lambda l, j: (0, 0)),
            pl.BlockSpec((None, d, tn), lambda l, j: (l, 0, j)),
            pl.BlockSpec((None, 1, tn), lambda l, j: (l, 0, j)),
        ],
        out_specs=pl.BlockSpec((None, COND_ROWS, tn), lambda l, j: (l, 0, j)),
        out_shape=jax.ShapeDtypeStruct((depth, COND_ROWS, nd), F32),
        compiler_params=_params(("parallel", "parallel")),
        name="ada",
    )(cond, w_ada, b_ada.reshape(depth, 1, nd))


def _ffn_kernel(x_ref, mod_ref, nw_ref, wg_ref, wu_ref, wd_ref, o_ref, *, mi):
    x = x_ref[...]
    h = _norm_mod(x, nw_ref[...], mod_ref[mi + 1:mi + 2, :], mod_ref[mi:mi + 1, :]).astype(BF16)
    g = _dot(h, wg_ref[...])
    u = _dot(h, wu_ref[...])
    a = (g * jax.nn.sigmoid(g) * u).astype(BF16)
    o_ref[...] = x + (0.5 * mod_ref[mi + 2:mi + 3, :]) * _dot(a, wd_ref[...])


def _ffn(x, mods, norm_w4, wg, wu, wd, l, j, group_of_tile, tm):
    n, d = x.shape
    dff = wg.shape[-1]
    mi = 0 if j == 0 else 6
    nwi = 0 if j == 0 else 2
    resident = lambda shape: pl.BlockSpec((None, None) + shape, lambda i: (l, j, 0, 0),
                                          pipeline_mode=pl.Buffered(1))
    return pl.pallas_call(
        functools.partial(_ffn_kernel, mi=mi),
        grid=(n // tm,),
        in_specs=[
            pl.BlockSpec((tm, d), lambda i: (i, 0)),
            pl.BlockSpec((None, None, N_MOD, d), lambda i: (l, group_of_tile(i), 0, 0)),
            pl.BlockSpec((None, None, 1, d), lambda i: (l, nwi, 0, 0)),
            resident((d, dff)), resident((d, dff)), resident((dff, d)),
        ],
        out_specs=pl.BlockSpec((tm, d), lambda i: (i, 0)),
        out_shape=jax.ShapeDtypeStruct((n, d), F32),
        compiler_params=_params(("parallel",)),
        name="ffn",
    )(x, mods, norm_w4, wg, wu, wd)


def _inproj_kernel(*refs, rope, ctx_out):
    it = iter(refs)
    (x_ref, mod_ref, nw_ref, w_ref, wgvt_ref, wmt_ref, qn_ref, kn_ref, gbi_ref, gbf_ref, mqn_ref, mkvn_ref, wq_ref,
     wk_ref, wvt_ref) = (next(it) for _ in range(15))
    if rope:
        cq_ref, sq_ref, cm_ref, sm_ref = (next(it) for _ in range(4))
    gq_o, gk_o, gv_o, mk_o, mqt_o, mkt_o, mvt_o, mo_o, gi_o, gf_o, qm_o, kp_o, vm_o = (next(it) for _ in range(13))
    if ctx_out:
        gkf_o, gvf_o, ckv_o, kr_o = (next(it) for _ in range(4))

    x = x_ref[...]
    tm = x.shape[0]
    hb = _norm_mod(x, nw_ref[...], mod_ref[4:5, :], mod_ref[3:4, :]).astype(BF16)

    def proj(o, n):
        return _dot(hb, w_ref[:, o:o + n])

    lane = lax.broadcasted_iota(jnp.int32, (tm, LANES), 1)

    def head_norm(xg, wrow):
        ms = jnp.sum(xg * xg, axis=-1, keepdims=True) * (1.0 / HEAD_DIM)
        return xg * lax.rsqrt(ms + EPS) * wrow

    def rope_hd(xg):
        half = HEAD_DIM // 2
        partner = jnp.where(lane < half, pltpu.roll(xg, LANES - half, 1), pltpu.roll(xg, half, 1))
        return xg * cq_ref[...] + partner * sq_ref[...]

    def rope_mla(xg):
        half = MLA_ROPE // 2
        partner = jnp.where((lane & half) == 0, pltpu.roll(xg, LANES - half, 1), pltpu.roll(xg, half, 1))
        return xg * cm_ref[...] + partner * sm_ref[...]

    gq = proj(O_GQ, GQA_HEADS * LANES)
    for g in range(GQA_HEADS):
        qg = head_norm(gq[:, g * LANES:(g + 1) * LANES], qn_ref[...])
        if rope:
            qg = rope_hd(qg)
        gq_o[:, g * LANES:(g + 1) * LANES] = (qg * (LOG2E * HEAD_DIM ** -0.5)).astype(BF16)
    gk = proj(O_GK, GQA_KV_HEADS * LANES)
    for g in range(GQA_KV_HEADS):
        kg = head_norm(gk[:, g * LANES:(g + 1) * LANES], kn_ref[...])
        if ctx_out:
            gkf_o[:, g * LANES:(g + 1) * LANES] = kg
        if rope:
            kg = rope_hd(kg)
        gk_o[:, g * LANES:(g + 1) * LANES] = kg.astype(BF16)
    gv_o[...] = _dot_nt(wgvt_ref[...], hb).astype(BF16)
    if ctx_out:
        gvf_o[...] = proj(O_GV, GQA_KV_HEADS * HEAD_DIM)

    mk_o[...] = proj(O_MK, ML_W).astype(BF16)
    mt = _dot_nt(wmt_ref[...], hb)
    mqt_o[...] = (mt[0:ML_W] * (ML_DK ** -0.5)).astype(BF16)
    mkt_o[...] = mt[ML_W:2 * ML_W].astype(BF16)
    mvt_o[...] = mt[2 * ML_W:3 * ML_W].astype(BF16)
    mo_o[...] = proj(O_MO, ML_W)
    gi_o[...] = proj(O_GI, LANES) + gbi_ref[...]
    gf_o[...] = proj(O_GF, LANES) + gbf_ref[...]

    xq = proj(O_QL, MLA_RANK)
    ms = jnp.mean(xq * xq, axis=-1, keepdims=True)
    qn = (xq * lax.rsqrt(ms + EPS) * mqn_ref[...]).astype(BF16)
    qm = _dot(qn, wq_ref[...])
    for g in range(MLA_HEADS):
        qg = qm[:, g * LANES:(g + 1) * LANES]
        if rope:
            qg = rope_mla(qg)
        qm_o[:, g * LANES:(g + 1) * LANES] = (qg * (LOG2E * MLA_QK ** -0.5)).astype(BF16)

    xkv = proj(O_KV, MLA_RANK)
    ms = jnp.mean(xkv * xkv, axis=-1, keepdims=True)
    ckv = xkv * lax.rsqrt(ms + EPS) * mkvn_ref[...]
    kr = proj(O_KR, LANES)
    if rope:
        kr = rope_mla(kr)
    if ctx_out:
        ckv_o[...] = ckv
        kr_o[...] = kr
    cb = ckv.astype(BF16)
    kk = _dot(cb, wk_ref[...])
    for g in range(MLA_HEADS):
        kp_o[:, g * LANES:(g + 1) * LANES] = (kk[:, g * LANES:(g + 1) * LANES] + kr).astype(BF16)
    vm_o[...] = _dot_nt(wvt_ref[...], cb).astype(BF16)


def _inproj(x, mods, norm_w4, w_in_p, wgvt, wmt, qn, kn, gbi, gbf, mqn, mkvn, wq, wk, wvt, rope_tabs, l, group_of_tile,
            tm, rows_per_seq, ctx_out):
    n, d = x.shape
    rope = rope_tabs is not None
    tiles_per_seq = rows_per_seq // tm
    row = lambda i: (i, 0)
    lay = lambda shape: pl.BlockSpec((None,) + shape, lambda i: (l,) + (0,) * len(shape))
    in_specs = [
        pl.BlockSpec((tm, d), row),
        pl.BlockSpec((None, None, N_MOD, d), lambda i: (l, group_of_tile(i), 0, 0)),
        pl.BlockSpec((None, None, 1, d), lambda i: (l, 1, 0, 0)),
        lay((d, D_INP)), lay((GQA_KV_HEADS * HEAD_DIM, d)), lay((3 * ML_W, d)),
        lay((1, LANES)), lay((1, LANES)), lay((1, LANES)), lay((1, LANES)), lay((1, MLA_RANK)), lay((1, MLA_RANK)),
        lay((MLA_RANK, MLA_HEADS * LANES)), lay((MLA_RANK, MLA_HEADS * LANES)), lay((MLA_HEADS * MLA_V, MLA_RANK)),
    ]
    args = [x, mods, norm_w4, w_in_p, wgvt, wmt, qn, kn, gbi, gbf, mqn, mkvn, wq, wk, wvt]
    if rope:
        tab = pl.BlockSpec((tm, LANES), lambda i: (i % tiles_per_seq, 0))
        in_specs += [tab] * 4
        args += list(rope_tabs)
    widths = [(GQA_HEADS * LANES, BF16, "row"), (GQA_KV_HEADS * LANES, BF16, "row"),
              (GQA_KV_HEADS * HEAD_DIM, BF16, "col"),
              (ML_W, BF16, "row"), (ML_W, BF16, "seq"), (ML_W, BF16, "seq"), (ML_W, BF16, "seq"),
              (ML_W, F32, "row"), (LANES, F32, "row"), (LANES, F32, "row"),
              (MLA_HEADS * LANES, BF16, "row"), (MLA_HEADS * LANES, BF16, "row"), (MLA_HEADS * MLA_V, BF16, "col")]
    if ctx_out:
        widths += [(GQA_KV_HEADS * LANES, F32, "row"), (GQA_KV_HEADS * HEAD_DIM, F32, "row"),
                   (MLA_RANK, F32, "row"), (LANES, F32, "row")]
    spec = {"row": lambda w: pl.BlockSpec((tm, w), row),
            "col": lambda w: pl.BlockSpec((w, tm), lambda i: (0, i)),
            "seq": lambda w: pl.BlockSpec((None, w, tm), lambda i: (i // tiles_per_seq, 0, i % tiles_per_seq))}
    shape = {"row": lambda w: (n, w), "col": lambda w: (w, n), "seq": lambda w: (n // rows_per_seq, w, rows_per_seq)}
    return pl.pallas_call(
        functools.partial(_inproj_kernel, rope=rope, ctx_out=ctx_out),
        grid=(n // tm,),
        in_specs=in_specs,
        out_specs=[spec[kind](w) for w, _, kind in widths],
        out_shape=[jax.ShapeDtypeStruct(shape[kind](w), dt) for w, dt, kind in widths],
        compiler_params=_params(("parallel",)),
        name="inproj",
    )(*args)


def _kvexp_kernel(ckv_ref, kr_ref, wk_ref, wvt_ref, kp_o, vm_o):
    cb = ckv_ref[...].astype(BF16)
    kk = _dot(cb, wk_ref[...])
    kr = kr_ref[...]
    for g in range(MLA_HEADS):
        kp_o[:, g * LANES:(g + 1) * LANES] = (kk[:, g * LANES:(g + 1) * LANES] + kr).astype(BF16)
    vm_o[...] = _dot_nt(wvt_ref[...], cb).astype(BF16)


def _kvexp(ckv, kr_slot, wk, wvt, l):
    n = ckv.shape[0]
    lay = lambda shape: pl.BlockSpec((None,) + shape, lambda i: (l,) + (0,) * len(shape))
    return pl.pallas_call(
        _kvexp_kernel,
        grid=(1,),
        in_specs=[pl.BlockSpec((n, MLA_RANK), lambda i: (0, 0)), pl.BlockSpec((n, LANES), lambda i: (0, 0)),
                  lay((MLA_RANK, MLA_HEADS * LANES)), lay((MLA_HEADS * MLA_V, MLA_RANK))],
        out_specs=[pl.BlockSpec((n, MLA_HEADS * LANES), lambda i: (0, 0)),
                   pl.BlockSpec((MLA_HEADS * MLA_V, n), lambda i: (0, 0))],
        out_shape=[jax.ShapeDtypeStruct((n, MLA_HEADS * LANES), BF16),
                   jax.ShapeDtypeStruct((MLA_HEADS * MLA_V, n), BF16)],
        compiler_params=_params(("arbitrary",)),
        name="kvexp",
    )(ckv, kr_slot, wk, wvt)


def _attn_kernel(*refs, n_src, heads, group):
    q_ref = refs[0]
    k_refs = refs[1:1 + n_src]
    vt_refs = refs[1 + n_src:1 + 2 * n_src]
    o_ref = refs[1 + 2 * n_src]
    head_outs = []

    def reduce_rows(x, op):
        r = x.shape[0]
        part = op(x.reshape(r // REDUCE_ROWS, REDUCE_ROWS, x.shape[1]), axis=0) if r > REDUCE_ROWS else x
        return op(part, axis=0, keepdims=True)

    for h in range(heads):
        kvh = h // group
        qh = q_ref[:, h * LANES:(h + 1) * LANES]
        sts = [_dot_nt(k[:, kvh * LANES:(kvh + 1) * LANES], qh) for k in k_refs]
        m = reduce_rows(sts[0], jnp.max)
        for st in sts[1:]:
            m = jnp.maximum(m, reduce_rows(st, jnp.max))
        den = None
        acc = None
        for st, vt in zip(sts, vt_refs):
            p = jnp.exp2(st - m)
            ps = reduce_rows(p, jnp.sum)
            pv = _dot(vt[kvh * MLA_V:(kvh + 1) * MLA_V, :], p.astype(BF16))
            den = ps if den is None else den + ps
            acc = pv if acc is None else acc + pv
        head_outs.append(acc / den)
    pairs = [jnp.concatenate(head_outs[i:i + 2], axis=0).T for i in range(0, heads, 2)]
    o_ref[...] = jnp.concatenate(pairs, axis=-1).astype(BF16)


def _attn(q, ks, vts, batch, heads, group, tq):
    n = q.shape[0]
    t = n // batch
    nq = t // tq
    n_src = len(ks)
    in_specs = [pl.BlockSpec((tq, heads * LANES), lambda b, i: (b * nq + i, 0))]
    for a in ks:
        in_specs.append(pl.BlockSpec((a.shape[0] // batch, a.shape[1]), lambda b, i: (b, 0)))
    for a in vts:
        in_specs.append(pl.BlockSpec((a.shape[0], a.shape[1] // batch), lambda b, i: (0, b)))
    return pl.pallas_call(
        functools.partial(_attn_kernel, n_src=n_src, heads=heads, group=group),
        grid=(batch, nq),
        in_specs=in_specs,
        out_specs=pl.BlockSpec((tq, heads * MLA_V), lambda b, i: (b * nq + i, 0)),
        out_shape=jax.ShapeDtypeStruct((n, heads * MLA_V), BF16),
        compiler_params=_params(("parallel", "parallel")),
        name="attn",
    )(q, *ks, *vts)


def _mxu_transpose(eye, parts):
    out = _dot_nt(eye, parts[0])
    for p in parts[1:]:
        out = out + _dot_nt(eye, p)
    return out


def _eye(n, m):
    return (lax.broadcasted_iota(jnp.int32, (n, m), 0) == lax.broadcasted_iota(jnp.int32, (n, m), 1)).astype(BF16)


def _mlstm_kernel(*refs, chunk, nc, nb, has_init, want_state):
    it = iter(refs)
    fwd = tuple(next(it) for _ in range(6))
    bwd = tuple(next(it) for _ in range(6))
    if has_init:
        c0_ref, n0_ref, m0_ref = (next(it) for _ in range(3))
    h_outs = (next(it), next(it))
    if want_state:
        c_out, n_out, m_out = (next(it) for _ in range(3))
    ct_s, n_s, m_s = (next(it) for _ in range(3))
    step = pl.program_id(1)
    L = chunk
    eye_l = _eye(L, L)
    eye_lanes = _eye(LANES, LANES)
    eye_dk = _eye(ML_DK, ML_DK)
    units = [(j, d, h) for j in range(nb) for d in range(2) for h in range(ML_HEADS)]

    @pl.when(step == 0)
    def _():
        if has_init:
            zpad = jnp.zeros((ML_DK, ML_DK), F32)
            for j, d, h in units:
                c0 = jnp.concatenate([c0_ref[j, d, h], zpad], axis=0)
                ct_s[j, d, h] = _mxu_transpose(eye_dk, _split3(c0))
                n_s[j, d, h] = jnp.concatenate([n0_ref[j, d, h:h + 1, :], zpad[0:1]], axis=1)
            m_s[...] = m0_ref[...]
        else:
            ct_s[...] = jnp.zeros_like(ct_s)
            n_s[...] = jnp.zeros_like(n_s)
            m_s[...] = jnp.zeros_like(m_s)

    prev = {u: (ct_s[u[0], u[1], u[2]], n_s[u[0], u[1], u[2]]) for u in units}
    m_prev = {(j, d): m_s[j, d] for j in range(nb) for d in range(2)}
    new, m_next = {}, {}

    row = lax.broadcasted_iota(jnp.int32, (L, L), 0)
    col = lax.broadcasted_iota(jnp.int32, (L, L), 1)
    sel_row = lax.broadcasted_iota(jnp.int32, (LANES, L), 0)
    head_of_lane = lax.broadcasted_iota(jnp.int32, (L, ML_W), 1) // ML_DK
    ones_rows = jnp.ones((2 * COND_ROWS, L), BF16)
    zero_rows = jnp.zeros((2 * COND_ROWS, L), BF16)
    zero_half = jnp.zeros((ML_DK, L), BF16)
    for j in range(nb):
        for d in range(2):
            k_ref, qt_ref, kt_ref, vt_ref, gi_ref, gf_ref = fwd if d == 0 else bwd
            tri = ((col <= row) if d == 0 else (col >= row)).astype(BF16)
            valid_t = (row <= col) if d == 0 else (row >= col)
            bcum = _dot_exact_lhs(tri, jax.nn.log_sigmoid(gf_ref[j]))
            a = gi_ref[j] - bcum
            m_vec = m_prev[(j, d)]
            b_last = bcum[L - 1:L, :] if d == 0 else bcum[0:1, :]
            mx = jnp.maximum(m_vec, jnp.max(a, axis=0, keepdims=True))
            m_next[(j, d)] = b_last + mx
            w_old = jnp.exp(m_vec - mx)
            inter_t = _mxu_transpose(eye_lanes, _split3(bcum + m_vec))
            w_t = _mxu_transpose(eye_lanes, _split3(jnp.exp(a - mx)))
            a_ext = jnp.concatenate([a - m_vec, jnp.broadcast_to(w_old, (2 * COND_ROWS, LANES))], axis=0)
            a_parts = _split3(a_ext)
            k_tok = k_ref[j]
            qt, kt, vt = qt_ref[j], kt_ref[j], vt_ref[j]
            hts = []
            for h in range(ML_HEADS):
                u = d * ML_HEADS + h
                ct, n_row = prev[(j, d, h)]
                pick = (sel_row == u).astype(BF16)
                abc = _dot(a_parts[0], pick) + _dot(a_parts[1], pick) + _dot(a_parts[2], pick)
                w_old_u = abc[L:L + 1, :]
                dm = jnp.where(valid_t, abc[0:L], -jnp.inf)
                mp = jnp.maximum(jnp.max(dm, axis=0, keepdims=True), 0.0)
                w_inter = jnp.exp(-mp)
                st = _dot(jnp.where(head_of_lane == h, k_tok, jnp.zeros_like(k_tok)), qt)
                qk_t = st * jnp.exp(dm - mp)
                q_h = qt[h * ML_DK:(h + 1) * ML_DK, :]
                v_h = vt[h * ML_DK:(h + 1) * ML_DK, :]
                rhs = jnp.concatenate([qk_t.astype(BF16), (q_h.astype(F32) * w_inter).astype(BF16), zero_half], axis=0)
                lhs = jnp.concatenate([
                    jnp.concatenate([v_h, ct.astype(BF16)], axis=1),
                    jnp.concatenate([zero_rows, jnp.broadcast_to(n_row, (2 * COND_ROWS, LANES)).astype(BF16)], axis=1),
                ], axis=0)
                res = _dot(lhs, rhs)
                den = res[ML_DK:ML_DK + 1] + jnp.sum(qk_t, axis=0, keepdims=True)
                hts.append(res[0:ML_DK] / jnp.maximum(jnp.abs(den), jnp.exp(-(inter_t[u:u + 1, :] + mp))))

                kw = (kt[h * ML_DK:(h + 1) * ML_DK, :].astype(F32) * w_t[u:u + 1, :]).astype(BF16)
                upd = _dot_nt(jnp.concatenate([v_h, ones_rows], axis=0),
                              jnp.concatenate([kw, zero_half], axis=0))
                new[(j, d, h)] = (w_old_u * ct + upd[0:ML_DK], w_old_u * n_row + upd[ML_DK:ML_DK + 1])
            ht_all = jnp.concatenate(hts, axis=0)
            h_outs[d][j] = _mxu_transpose(eye_l, _split3(ht_all))

    for (j, d, h), (ct_new, n_new) in new.items():
        ct_s[j, d, h] = ct_new
        n_s[j, d, h] = n_new
    for (j, d), m_new in m_next.items():
        m_s[j, d] = m_new

    if want_state:
        @pl.when(step == nc - 1)
        def _():
            pad_eye = _eye(ML_DK, LANES)
            for j, d, h in units:
                c_out[j, d, h] = _mxu_transpose(pad_eye, _split3(new[(j, d, h)][0]))
                n_out[j, d, h:h + 1, :] = new[(j, d, h)][1][:, 0:ML_DK]
            for (j, d), m_new in m_next.items():
                m_out[j, d] = m_new


def _mlstm(mk, mqt, mkt, mvt, gi, gf, batch, chunk, nb, init, l, want_state):
    t = mk.shape[0] // batch
    nc = t // chunk
    seq = lambda a: a.reshape(batch, t, a.shape[-1])
    in_specs, args = [], []
    for cidx in (lambda c: c, lambda c: nc - 1 - c):
        tok = lambda b, c, cidx=cidx: (b, cidx(c), 0)
        tr = lambda b, c, cidx=cidx: (b, 0, cidx(c))
        in_specs += [pl.BlockSpec((nb, chunk, ML_W), tok)] + [pl.BlockSpec((nb, ML_W, chunk), tr)] * 3
        in_specs += [pl.BlockSpec((nb, chunk, LANES), tok)] * 2
        args += [seq(mk), mqt, mkt, mvt, seq(gi), seq(gf)]
    has_init = init is not None
    if has_init:
        in_specs += [
            pl.BlockSpec((nb, None, 2, ML_HEADS, ML_DK, ML_DK), lambda b, c: (b, l, 0, 0, 0, 0)),
            pl.BlockSpec((nb, None, 2, ML_HEADS, ML_DK), lambda b, c: (b, l, 0, 0, 0)),
            pl.BlockSpec((nb, None, 2, 1, LANES), lambda b, c: (b, l, 0, 0, 0)),
        ]
        args += list(init)
    out_specs = [pl.BlockSpec((nb, chunk, ML_W), lambda b, c: (b, c, 0)),
                 pl.BlockSpec((nb, chunk, ML_W), lambda b, c: (b, nc - 1 - c, 0))]
    out_shape = [jax.ShapeDtypeStruct((batch, t, ML_W), F32)] * 2
    if want_state:
        out_specs += [
            pl.BlockSpec((nb, 2, ML_HEADS, ML_DK, ML_DK), lambda b, c: (b, 0, 0, 0, 0)),
            pl.BlockSpec((nb, 2, ML_HEADS, ML_DK), lambda b, c: (b, 0, 0, 0)),
            pl.BlockSpec((nb, 2, 1, LANES), lambda b, c: (b, 0, 0, 0)),
        ]
        out_shape += [jax.ShapeDtypeStruct((batch, 2, ML_HEADS, ML_DK, ML_DK), F32),
                      jax.ShapeDtypeStruct((batch, 2, ML_HEADS, ML_DK), F32),
                      jax.ShapeDtypeStruct((batch, 2, 1, LANES), F32)]
    outs = pl.pallas_call(
        functools.partial(_mlstm_kernel, chunk=chunk, nc=nc, nb=nb, has_init=has_init, want_state=want_state),
        grid=(batch // nb, nc),
        in_specs=in_specs,
        out_specs=out_specs,
        out_shape=out_shape,
        scratch_shapes=[pltpu.VMEM((nb, 2, ML_HEADS, ML_DK, LANES), F32), pltpu.VMEM((nb, 2, ML_HEADS, 1, LANES), F32),
                        pltpu.VMEM((nb, 2, 1, LANES), F32)],
        compiler_params=_params(("parallel", "arbitrary")),
        name="mlstm",
    )(*args)
    return [outs[0].reshape(batch * t, ML_W), outs[1].reshape(batch * t, ML_W)] + list(outs[2:])


def _outproj_kernel(x_ref, mod_ref, go_ref, hf_ref, hb_ref, mo_ref, lo_ref, onw_ref, seg_ref, w_ref, o_ref):
    gqa_w = GQA_HEADS * HEAD_DIM
    hsum = hf_ref[...] + hb_ref[...]
    hi, mid, lo = _split3(hsum * hsum)
    seg = seg_ref[...]
    ms = _dot(hi, seg) + _dot(mid, seg) + _dot(lo, seg)
    hn = hsum * lax.rsqrt(ms + EPS) * onw_ref[...]
    ml = (hn * jax.nn.sigmoid(mo_ref[...])).astype(BF16)
    out = (_dot(go_ref[...], w_ref[0:gqa_w, :]) + _dot(ml, w_ref[gqa_w:gqa_w + ML_W, :])
           + _dot(lo_ref[...], w_ref[gqa_w + ML_W:, :]))
    o_ref[...] = x_ref[...] + mod_ref[5:6, :] * out


def _outproj(x, mods, gqa_o, hf, hb, mo, mla_o, onw, seg, w_out, l, group_of_tile, tm):
    n, d = x.shape
    row = lambda i: (i, 0)
    lay = lambda shape: pl.BlockSpec((None,) + shape, lambda i: (l,) + (0,) * len(shape))
    return pl.pallas_call(
        _outproj_kernel,
        grid=(n // tm,),
        in_specs=[
            pl.BlockSpec((tm, d), row),
            pl.BlockSpec((None, None, N_MOD, d), lambda i: (l, group_of_tile(i), 0, 0)),
            pl.BlockSpec((tm, GQA_HEADS * HEAD_DIM), row),
            pl.BlockSpec((tm, ML_W), row), pl.BlockSpec((tm, ML_W), row), pl.BlockSpec((tm, ML_W), row),
            pl.BlockSpec((tm, MLA_HEADS * MLA_V), row),
            lay((1, ML_W)),
            pl.BlockSpec((ML_W, ML_W), lambda i: (0, 0)),
            lay((w_out.shape[1], d)),
        ],
        out_specs=pl.BlockSpec((tm, d), row),
        out_shape=jax.ShapeDtypeStruct((n, d), F32),
        compiler_params=_params(("parallel",)),
        name="outproj",
    )(x, mods, gqa_o, hf, hb, mo, mla_o, onw, seg, w_out)


def _final_norm_kernel(x_ref, w_ref, o_ref):
    x = x_ref[...]
    ms = jnp.mean(x * x, axis=-1, keepdims=True)
    o_ref[...] = x * lax.rsqrt(ms + EPS) * w_ref[...]


def _final_norm(x, w, tm):
    n, d = x.shape
    return pl.pallas_call(
        _final_norm_kernel,
        grid=(n // tm,),
        in_specs=[pl.BlockSpec((tm, d), lambda i: (i, 0)), pl.BlockSpec((1, d), lambda i: (0, 0))],
        out_specs=pl.BlockSpec((tm, d), lambda i: (i, 0)),
        out_shape=jax.ShapeDtypeStruct((n, d), F32),
        compiler_params=_params(("parallel",)),
        name="final_norm",
    )(x, w.reshape(1, d))


def _pad_last(a, width):
    return jnp.pad(a, [(0, 0)] * (a.ndim - 1) + [(0, width - a.shape[-1])])


def _pad_heads(a, heads, width):
    lead = a.shape[:-1]
    return _pad_last(a.reshape(lead + (heads, width)), LANES).reshape(lead + (heads * LANES,))


def _axial_rope(seq, rot_dim):
    half = rot_dim // 2
    rows = seq // GRID_W
    freqs = ROPE_BASE ** (-jnp.arange(0, half, 2, dtype=F32) / half)
    r = jnp.repeat(jnp.arange(rows, dtype=F32), GRID_W)
    c = jnp.tile(jnp.arange(GRID_W, dtype=F32), rows)
    ang = jnp.concatenate([r[:, None] * freqs, c[:, None] * freqs], axis=-1)
    return jnp.cos(ang), jnp.sin(ang)


def _rope_tables(seq):
    c, s = _axial_rope(seq, HEAD_DIM)
    ones = jnp.ones((seq, LANES - HEAD_DIM), F32)
    cq = jnp.concatenate([c, c, ones], axis=-1)
    sq = jnp.concatenate([-s, s, 0.0 * ones], axis=-1)
    c, s = _axial_rope(seq, MLA_ROPE)
    one_a = jnp.ones((seq, MLA_NOPE), F32)
    one_b = jnp.ones((seq, LANES - MLA_QK), F32)
    cm = jnp.concatenate([one_a, c, c, one_b], axis=-1)
    sm = jnp.concatenate([0.0 * one_a, -s, s, 0.0 * one_b], axis=-1)
    return cq, sq, cm, sm


def _pick_tile(n, pref):
    t = min(n, pref)
    while n % t:
        t //= 2
    return t


def kernel(x_prompt, x_sample, c, cache_gqa_k, cache_gqa_v, cache_mla_ckv, cache_mla_krope, state_mlstm_C, state_mlstm_n, state_mlstm_m, c_ctx, w_ada, b_ada, norm_w, ffn_w_gate, ffn_w_up, ffn_w_down, w_in, gqa_q_norm, gqa_k_norm, mlstm_gate_b, mlstm_out_norm, mla_q_norm, mla_w_uq, mla_kv_norm, mla_w_ukv, w_out, final_norm):
    bc, tc, d = x_prompt.shape
    bl, tl, _ = x_sample.shape
    depth = w_ada.shape[0]
    past = cache_gqa_k.shape[2]
    assert 1 + bl <= COND_ROWS

    cond = jnp.concatenate([c_ctx[None, :], c, jnp.zeros((COND_ROWS - 1 - bl, d), F32)], axis=0)
    mods = _ada(cond, w_ada, b_ada).reshape(depth, COND_ROWS, N_MOD, d)

    wg, wu, wd = ffn_w_gate.astype(BF16), ffn_w_up.astype(BF16), ffn_w_down.astype(BF16)
    idx = [0]
    for wdt in (GQA_HEADS * HEAD_DIM, GQA_KV_HEADS * HEAD_DIM, GQA_KV_HEADS * HEAD_DIM, ML_W, ML_W, ML_W, ML_W,
                4 * ML_HEADS, MLA_RANK, MLA_RANK, MLA_ROPE):
        idx.append(idx[-1] + wdt)
    seg = [w_in[..., a:b] for a, b in zip(idx[:-1], idx[1:])]
    s_gq, s_gk, s_gv, s_mq, s_mk, s_mv, s_mo, s_mg, s_ql, s_kv, s_kr = seg
    kr_slot_w = jnp.pad(s_kr, [(0, 0), (0, 0), (MLA_NOPE, LANES - MLA_QK)])
    nh = ML_HEADS
    gi_slot_w = _pad_last(jnp.concatenate([s_mg[..., 0:nh], s_mg[..., 2 * nh:3 * nh]], axis=-1), LANES)
    gf_slot_w = _pad_last(jnp.concatenate([s_mg[..., nh:2 * nh], s_mg[..., 3 * nh:4 * nh]], axis=-1), LANES)
    w_in_p = jnp.concatenate(
        [_pad_heads(s_gq, GQA_HEADS, HEAD_DIM), _pad_heads(s_gk, GQA_KV_HEADS, HEAD_DIM), s_gv,
         s_mk, s_mo, s_ql, s_kv, kr_slot_w, gi_slot_w, gf_slot_w], axis=-1).astype(BF16)
    assert w_in_p.shape[-1] == D_INP
    wmt = jnp.swapaxes(jnp.concatenate([s_mq, s_mk, s_mv], axis=-1), 1, 2).astype(BF16)
    wq = _pad_heads(mla_w_uq, MLA_HEADS, MLA_QK).astype(BF16)
    ukv = mla_w_ukv.reshape(depth, MLA_RANK, MLA_HEADS, MLA_NOPE + MLA_V)
    wk = _pad_last(ukv[..., :MLA_NOPE], LANES).reshape(depth, MLA_RANK, MLA_HEADS * LANES).astype(BF16)
    wvt = jnp.swapaxes(ukv[..., MLA_NOPE:].reshape(depth, MLA_RANK, MLA_HEADS * MLA_V), 1, 2).astype(BF16)
    wgvt = jnp.swapaxes(s_gv, 1, 2).astype(BF16)
    w_out_b = w_out.astype(BF16)
    qn = _pad_last(gqa_q_norm, LANES).reshape(depth, 1, LANES)
    kn = _pad_last(gqa_k_norm, LANES).reshape(depth, 1, LANES)
    gbi = _pad_last(jnp.concatenate([mlstm_gate_b[:, 0], mlstm_gate_b[:, 2]], axis=-1), LANES).reshape(depth, 1, LANES)
    gbf = _pad_last(jnp.concatenate([mlstm_gate_b[:, 1], mlstm_gate_b[:, 3]], axis=-1), LANES).reshape(depth, 1, LANES)
    mqn = mla_q_norm.reshape(depth, 1, MLA_RANK)
    mkvn = mla_kv_norm.reshape(depth, 1, MLA_RANK)
    onw = mlstm_out_norm.reshape(depth, 1, ML_W)
    norm_w4 = norm_w.reshape(depth, 3, 1, d)
    head_id = jnp.arange(ML_W) // ML_DK
    seg_mean = ((head_id[:, None] == head_id[None, :]).astype(F32) / ML_DK).astype(BF16)
    rope_tabs = _rope_tables(tl)

    ck_gqa = _pad_heads(cache_gqa_k.reshape(bl, depth, past, GQA_KV_HEADS * HEAD_DIM), GQA_KV_HEADS,
                        HEAD_DIM).astype(BF16)
    cvt_gqa = jnp.transpose(cache_gqa_v.reshape(bl * depth, past, GQA_KV_HEADS * HEAD_DIM), (2, 0, 1)).reshape(
        GQA_KV_HEADS * HEAD_DIM, bl, depth, past).astype(BF16)
    ckr_slot = jnp.pad(cache_mla_krope, [(0, 0), (0, 0), (0, 0), (MLA_NOPE, LANES - MLA_QK)])
    m0_lanes = jnp.stack([jnp.pad(state_mlstm_m[:, :, dd], [(0, 0), (0, 0), (dd * nh, LANES - (dd + 1) * nh)])
                          for dd in range(2)], axis=2)[:, :, :, None, :]

    nc_rows, nl_rows = bc * tc, bl * tl
    tm_c, tm_l = _pick_tile(nc_rows, 512), _pick_tile(tl, 512)
    tm_ci = _pick_tile(tc, 512)
    tq_c, tq_l = _pick_tile(tc, 512), _pick_tile(tl, 512)
    chunk_c = chunk_l = LANES
    assert tc % LANES == 0 and tl % LANES == 0
    nb_c, nb_l = _pick_tile(bc, 2), _pick_tile(bl, 2)
    ctx_group = lambda i: 0
    lat_group = lambda i: 1 + i // (tl // tm_l)

    xc = x_prompt.reshape(nc_rows, d)
    xl = x_sample.reshape(nl_rows, d)
    collected = [[] for _ in range(7)]
    for l in range(depth):
        xc = _ffn(xc, mods, norm_w4, wg, wu, wd, l, 0, ctx_group, tm_c)
        xl = _ffn(xl, mods, norm_w4, wg, wu, wd, l, 0, lat_group, tm_l)

        (gq, gk, gv, mk, mqt, mkt, mvt, mo, gi, gf, qm, kp, vm, gkf, gvf, ckv, krs) = _inproj(
            xc, mods, norm_w4, w_in_p, wgvt, wmt, qn, kn, gbi, gbf, mqn, mkvn, wq, wk, wvt, None, l, ctx_group, tm_ci,
            tc, True)
        gqa_o = _attn(gq, [gk], [gv], bc, GQA_HEADS, GQA_GROUP, tq_c)
        mla_o = _attn(qm, [kp], [vm], bc, MLA_HEADS, 1, tq_c)
        hf, hb, c_new, n_new, m_lanes = _mlstm(mk, mqt, mkt, mvt, gi, gf, bc, chunk_c, nb_c, None, l, True)
        m_new = jnp.stack([m_lanes[:, dd, 0, dd * nh:(dd + 1) * nh] for dd in range(2)], axis=1)
        xc = _outproj(xc, mods, gqa_o, hf, hb, mo, mla_o, onw, seg_mean, w_out_b, l, ctx_group, tm_c)
        new_k = gkf.reshape(bc, tc, GQA_KV_HEADS, LANES)[..., :HEAD_DIM]
        new_v = gvf.reshape(bc, tc, GQA_KV_HEADS, HEAD_DIM)
        new_ckv = ckv.reshape(bc, tc, MLA_RANK)
        new_kr = krs.reshape(bc, tc, LANES)[..., MLA_NOPE:MLA_QK]
        for lst, t in zip(collected, (new_k, new_v, new_ckv, new_kr, c_new, n_new, m_new)):
            lst.append(t)

        (gq, gk, gv, mk, mqt, mkt, mvt, mo, gi, gf, qm, kp, vm) = _inproj(
            xl, mods, norm_w4, w_in_p, wgvt, wmt, qn, kn, gbi, gbf, mqn, mkvn, wq, wk, wvt, rope_tabs, l, lat_group,
            tm_l, tl, False)
        kp_c, vm_c = _kvexp(cache_mla_ckv[:, l].reshape(bl * past, MLA_RANK), ckr_slot[:, l].reshape(bl * past, LANES),
                            wk, wvt, l)
        gqa_o = _attn(gq, [ck_gqa[:, l].reshape(bl * past, -1), gk], [cvt_gqa[:, :, l].reshape(-1, bl * past), gv],
                      bl, GQA_HEADS, GQA_GROUP, tq_l)
        mla_o = _attn(qm, [kp_c, kp], [vm_c, vm], bl, MLA_HEADS, 1, tq_l)
        hf, hb = _mlstm(mk, mqt, mkt, mvt, gi, gf, bl, chunk_l, nb_l, (state_mlstm_C, state_mlstm_n, m0_lanes), l,
                        False)
        xl = _outproj(xl, mods, gqa_o, hf, hb, mo, mla_o, onw, seg_mean, w_out_b, l, lat_group, tm_l)

        xc = _ffn(xc, mods, norm_w4, wg, wu, wd, l, 1, ctx_group, tm_c)
        xl = _ffn(xl, mods, norm_w4, wg, wu, wd, l, 1, lat_group, tm_l)

    y_prompt = _final_norm(xc, final_norm, tm_c).reshape(bc, tc, d)
    y_sample = _final_norm(xl, final_norm, tm_l).reshape(bl, tl, d)
    stacked = [jnp.stack(lst, axis=1) for lst in collected]
    return (y_prompt, y_sample, *stacked)
```

```python
import functools

import jax
import jax.numpy as jnp
from jax import lax
from jax.experimental import pallas as pl
from jax.experimental.pallas import tpu as pltpu

F32 = jnp.float32
BF16 = jnp.bfloat16

EPS = 1e-6
ROPE_BASE = 10000.0
GRID_W = 64
N_MOD = 9
HEAD_DIM = 64
GQA_HEADS = 6
GQA_KV_HEADS = 2
GQA_GROUP = GQA_HEADS // GQA_KV_HEADS
ML_HEADS = 4
ML_DK = 64
ML_W = ML_HEADS * ML_DK
MLA_HEADS = 6
MLA_RANK = 256
MLA_NOPE = 64
MLA_ROPE = 32
MLA_V = 64
MLA_QK = MLA_NOPE + MLA_ROPE
LANES = 128
COND_ROWS = 8
VMEM_LIMIT = 48 * 1024 * 1024
LOG2E = 1.4426950408889634
REDUCE_ROWS = 64

O_GQ = 0
O_GK = O_GQ + GQA_HEADS * LANES
O_GV = O_GK + GQA_KV_HEADS * LANES
O_MK = O_GV + GQA_KV_HEADS * HEAD_DIM
O_MO = O_MK + ML_W
O_QL = O_MO + ML_W
O_KV = O_QL + MLA_RANK
O_KR = O_KV + MLA_RANK
O_GI = O_KR + LANES
O_GF = O_GI + LANES
D_INP = O_GF + LANES


def _params(sem):
    return pltpu.CompilerParams(dimension_semantics=sem, vmem_limit_bytes=VMEM_LIMIT)


def _norm_mod(x, nw, sc, sh):
    ms = jnp.mean(x * x, axis=-1, keepdims=True)
    return (x * lax.rsqrt(ms + EPS) * nw) * (1.0 + sc) + sh


def _dot(a, b):
    return jnp.dot(a, b, preferred_element_type=F32)


def _dot_nt(a, b):
    return lax.dot_general(a, b, (((1,), (1,)), ((), ())), preferred_element_type=F32)


def _dot_tn(a, b):
    return lax.dot_general(a, b, (((0,), (0,)), ((), ())), preferred_element_type=F32)


def _split3(x):
    hi = x.astype(BF16)
    r1 = x - hi.astype(F32)
    mid = r1.astype(BF16)
    lo = (r1 - mid.astype(F32)).astype(BF16)
    return hi, mid, lo


def _dot_exact_lhs(a_bf16, x):
    hi, mid, lo = _split3(x)
    return _dot(a_bf16, hi) + _dot(a_bf16, mid) + _dot(a_bf16, lo)


def _ada_kernel(cond_ref, w_ref, b_ref, o_ref):
    c = cond_ref[...]
    s = (c * jax.nn.sigmoid(c)).astype(BF16)
    o_ref[...] = _dot(s, w_ref[...].astype(BF16)) + b_ref[...]


def _ada(cond, w_ada, b_ada):
    depth, d, nd = w_ada.shape
    tn = d
    return pl.pallas_call(
        _ada_kernel,
        grid=(depth, nd // tn),
        in_specs=[
            pl.BlockSpec((COND_ROWS, d), lambda l, j: (0, 0)),
            pl.BlockSpec((None, d, tn), lambda l, j: (l, 0, j)),
            pl.BlockSpec((None, 1, tn), lambda l, j: (l, 0, j)),
        ],
        out_specs=pl.BlockSpec((None, COND_ROWS, tn), lambda l, j: (l, 0, j)),
        out_shape=jax.ShapeDtypeStruct((depth, COND_ROWS, nd), F32),
        compiler_params=_params(("parallel", "parallel")),
        name="ada",
    )(cond, w_ada, b_ada.reshape(depth, 1, nd))


def _ffn_kernel(x_ref, mod_ref, nw_ref, wg_ref, wu_ref, wd_ref, o_ref, *, mi):
    x = x_ref[...]
    h = _norm_mod(x, nw_ref[...], mod_ref[mi + 1:mi + 2, :], mod_ref[mi:mi + 1, :]).astype(BF16)
    g = _dot(h, wg_ref[...])
    u = _dot(h, wu_ref[...])
    a = (g * jax.nn.sigmoid(g) * u).astype(BF16)
    o_ref[...] = x + (0.5 * mod_ref[mi + 2:mi + 3, :]) * _dot(a, wd_ref[...])


def _ffn(x, mods, norm_w4, wg, wu, wd, l, j, group_of_tile, tm):
    n, d = x.shape
    dff = wg.shape[-1]
    mi = 0 if j == 0 else 6
    nwi = 0 if j == 0 else 2
    resident = lambda shape: pl.BlockSpec((None, None) + shape, lambda i: (l, j, 0, 0),
                                          pipeline_mode=pl.Buffered(1))
    return pl.pallas_call(
        functools.partial(_ffn_kernel, mi=mi),
        grid=(n // tm,),
        in_specs=[
            pl.BlockSpec((tm, d), lambda i: (i, 0)),
            pl.BlockSpec((None, None, N_MOD, d), lambda i: (l, group_of_tile(i), 0, 0)),
            pl.BlockSpec((None, None, 1, d), lambda i: (l, nwi, 0, 0)),
            resident((d, dff)), resident((d, dff)), resident((dff, d)),
        ],
        out_specs=pl.BlockSpec((tm, d), lambda i: (i, 0)),
        out_shape=jax.ShapeDtypeStruct((n, d), F32),
        compiler_params=_params(("parallel",)),
        name="ffn",
    )(x, mods, norm_w4, wg, wu, wd)


def _inproj_kernel(*refs, rope, ctx_out):
    it = iter(refs)
    (x_ref, mod_ref, nw_ref, w_ref, wgvt_ref, wmt_ref, qn_ref, kn_ref, gbi_ref, gbf_ref, mqn_ref, mkvn_ref, wq_ref,
     wk_ref, wvt_ref) = (next(it) for _ in range(15))
    if rope:
        cq_ref, sq_ref, cm_ref, sm_ref = (next(it) for _ in range(4))
    gq_o, gk_o, gv_o, mk_o, mqt_o, mkt_o, mvt_o, mo_o, gi_o, gf_o, qm_o, kp_o, vm_o = (next(it) for _ in range(13))
    if ctx_out:
        gkf_o, gvf_o, ckv_o, kr_o = (next(it) for _ in range(4))

    x = x_ref[...]
    tm = x.shape[0]
    hb = _norm_mod(x, nw_ref[...], mod_ref[4:5, :], mod_ref[3:4, :]).astype(BF16)

    def proj(o, n):
        return _dot(hb, w_ref[:, o:o + n])

    lane = lax.broadcasted_iota(jnp.int32, (tm, LANES), 1)

    def head_norm(xg, wrow):
        ms = jnp.sum(xg * xg, axis=-1, keepdims=True) * (1.0 / HEAD_DIM)
        return xg * lax.rsqrt(ms + EPS) * wrow

    def rope_hd(xg):
        half = HEAD_DIM // 2
        partner = jnp.where(lane < half, pltpu.roll(xg, LANES - half, 1), pltpu.roll(xg, half, 1))
        return xg * cq_ref[...] + partner * sq_ref[...]

    def rope_mla(xg):
        half = MLA_ROPE // 2
        partner = jnp.where((lane & half) == 0, pltpu.roll(xg, LANES - half, 1), pltpu.roll(xg, half, 1))
        return xg * cm_ref[...] + partner * sm_ref[...]

    gq = proj(O_GQ, GQA_HEADS * LANES)
    for g in range(GQA_HEADS):
        qg = head_norm(gq[:, g * LANES:(g + 1) * LANES], qn_ref[...])
        if rope:
            qg = rope_hd(qg)
        gq_o[:, g * LANES:(g + 1) * LANES] = (qg * (LOG2E * HEAD_DIM ** -0.5)).astype(BF16)
    gk = proj(O_GK, GQA_KV_HEADS * LANES)
    for g in range(GQA_KV_HEADS):
        kg = head_norm(gk[:, g * LANES:(g + 1) * LANES], kn_ref[...])
        if ctx_out:
            gkf_o[:, g * LANES:(g + 1) * LANES] = kg
        if rope:
            kg = rope_hd(kg)
        gk_o[:, g * LANES:(g + 1) * LANES] = kg.astype(BF16)
    gv_o[...] = _dot_nt(wgvt_ref[...], hb).astype(BF16)
    if ctx_out:
        gvf_o[...] = proj(O_GV, GQA_KV_HEADS * HEAD_DIM)

    mk_o[...] = proj(O_MK, ML_W).astype(BF16)
    mt = _dot_nt(wmt_ref[...], hb)
    mqt_o[...] = (mt[0:ML_W] * (ML_DK ** -0.5)).astype(BF16)
    mkt_o[...] = mt[ML_W:2 * ML_W].astype(BF16)
    mvt_o[...] = mt[2 * ML_W:3 * ML_W].astype(BF16)
    mo_o[...] = proj(O_MO, ML_W)
    gi_o[...] = proj(O_GI, LANES) + gbi_ref[...]
    gf_o[...] = proj(O_GF, LANES) + gbf_ref[...]

    xq = proj(O_QL, MLA_RANK)
    ms = jnp.mean(xq * xq, axis=-1, keepdims=True)
    qn = (xq * lax.rsqrt(ms + EPS) * mqn_ref[...]).astype(BF16)
    qm = _dot(qn, wq_ref[...])
    for g in range(MLA_HEADS):
        qg = qm[:, g * LANES:(g + 1) * LANES]
        if rope:
            qg = rope_mla(qg)
        qm_o[:, g * LANES:(g + 1) * LANES] = (qg * (LOG2E * MLA_QK ** -0.5)).astype(BF16)

    xkv = proj(O_KV, MLA_RANK)
    ms = jnp.mean(xkv * xkv, axis=-1, keepdims=True)
    ckv = xkv * lax.rsqrt(ms + EPS) * mkvn_ref[...]
    kr = proj(O_KR, LANES)
    if rope:
        kr = rope_mla(kr)
    if ctx_out:
        ckv_o[...] = ckv
        kr_o[...] = kr
    cb = ckv.astype(BF16)
    kk = _dot(cb, wk_ref[...])
    for g in range(MLA_HEADS):
        kp_o[:, g * LANES:(g + 1) * LANES] = (kk[:, g * LANES:(g + 1) * LANES] + kr).astype(BF16)
    vm_o[...] = _dot_nt(wvt_ref[...], cb).astype(BF16)


def _inproj(x, mods, norm_w4, w_in_p, wgvt, wmt, qn, kn, gbi, gbf, mqn, mkvn, wq, wk, wvt, rope_tabs, l, group_of_tile,
            tm, rows_per_seq, ctx_out):
    n, d = x.shape
    rope = rope_tabs is not None
    tiles_per_seq = rows_per_seq // tm
    row = lambda i: (i, 0)
    lay = lambda shape: pl.BlockSpec((None,) + shape, lambda i: (l,) + (0,) * len(shape))
    in_specs = [
        pl.BlockSpec((tm, d), row),
        pl.BlockSpec((None, None, N_MOD, d), lambda i: (l, group_of_tile(i), 0, 0)),
        pl.BlockSpec((None, None, 1, d), lambda i: (l, 1, 0, 0)),
        lay((d, D_INP)), lay((GQA_KV_HEADS * HEAD_DIM, d)), lay((3 * ML_W, d)),
        lay((1, LANES)), lay((1, LANES)), lay((1, LANES)), lay((1, LANES)), lay((1, MLA_RANK)), lay((1, MLA_RANK)),
        lay((MLA_RANK, MLA_HEADS * LANES)), lay((MLA_RANK, MLA_HEADS * LANES)), lay((MLA_HEADS * MLA_V, MLA_RANK)),
    ]
    args = [x, mods, norm_w4, w_in_p, wgvt, wmt, qn, kn, gbi, gbf, mqn, mkvn, wq, wk, wvt]
    if rope:
        tab = pl.BlockSpec((tm, LANES), lambda i: (i % tiles_per_seq, 0))
        in_specs += [tab] * 4
        args += list(rope_tabs)
    widths = [(GQA_HEADS * LANES, BF16, "row"), (GQA_KV_HEADS * LANES, BF16, "row"),
              (GQA_KV_HEADS * HEAD_DIM, BF16, "col"),
              (ML_W, BF16, "row"), (ML_W, BF16, "seq"), (ML_W, BF16, "seq"), (ML_W, BF16, "seq"),
              (ML_W, F32, "row"), (LANES, F32, "row"), (LANES, F32, "row"),
              (MLA_HEADS * LANES, BF16, "row"), (MLA_HEADS * LANES, BF16, "row"), (MLA_HEADS * MLA_V, BF16, "col")]
    if ctx_out:
        widths += [(GQA_KV_HEADS * LANES, F32, "row"), (GQA_KV_HEADS * HEAD_DIM, F32, "row"),
                   (MLA_RANK, F32, "row"), (LANES, F32, "row")]
    spec = {"row": lambda w: pl.BlockSpec((tm, w), row),
            "col": lambda w: pl.BlockSpec((w, tm), lambda i: (0, i)),
            "seq": lambda w: pl.BlockSpec((None, w, tm), lambda i: (i // tiles_per_seq, 0, i % tiles_per_seq))}
    shape = {"row": lambda w: (n, w), "col": lambda w: (w, n), "seq": lambda w: (n // rows_per_seq, w, rows_per_seq)}
    return pl.pallas_call(
        functools.partial(_inproj_kernel, rope=rope, ctx_out=ctx_out),
        grid=(n // tm,),
        in_specs=in_specs,
        out_specs=[spec[kind](w) for w, _, kind in widths],
        out_shape=[jax.ShapeDtypeStruct(shape[kind](w), dt) for w, dt, kind in widths],
        compiler_params=_params(("parallel",)),
        name="inproj",
    )(*args)


def _kvexp_kernel(ckv_ref, kr_ref, wk_ref, wvt_ref, kp_o, vm_o):
    cb = ckv_ref[...].astype(BF16)
    kk = _dot(cb, wk_ref[...])
    kr = kr_ref[...]
    for g in range(MLA_HEADS):
        kp_o[:, g * LANES:(g + 1) * LANES] = (kk[:, g * LANES:(g + 1) * LANES] + kr).astype(BF16)
    vm_o[...] = _dot_nt(wvt_ref[...], cb).astype(BF16)


def _kvexp(ckv, kr_slot, wk, wvt, l):
    n = ckv.shape[0]
    lay = lambda shape: pl.BlockSpec((None,) + shape, lambda i: (l,) + (0,) * len(shape))
    return pl.pallas_call(
        _kvexp_kernel,
        grid=(1,),
        in_specs=[pl.BlockSpec((n, MLA_RANK), lambda i: (0, 0)), pl.BlockSpec((n, LANES), lambda i: (0, 0)),
                  lay((MLA_RANK, MLA_HEADS * LANES)), lay((MLA_HEADS * MLA_V, MLA_RANK))],
        out_specs=[pl.BlockSpec((n, MLA_HEADS * LANES), lambda i: (0, 0)),
                   pl.BlockSpec((MLA_HEADS * MLA_V, n), lambda i: (0, 0))],
        out_shape=[jax.ShapeDtypeStruct((n, MLA_HEADS * LANES), BF16),
                   jax.ShapeDtypeStruct((MLA_HEADS * MLA_V, n), BF16)],
        compiler_params=_params(("arbitrary",)),
        name="kvexp",
    )(ckv, kr_slot, wk, wvt)


def _attn_kernel(*refs, n_src, heads, group):
    q_ref = refs[0]
    k_refs = refs[1:1 + n_src]
    vt_refs = refs[1 + n_src:1 + 2 * n_src]
    o_ref = refs[1 + 2 * n_src]
    head_outs = []

    def reduce_rows(x, op):
        r = x.shape[0]
        part = op(x.reshape(r // REDUCE_ROWS, REDUCE_ROWS, x.shape[1]), axis=0) if r > REDUCE_ROWS else x
        return op(part, axis=0, keepdims=True)

    def scores(h):
        kvh = h // group
        qh = q_ref[:, h * LANES:(h + 1) * LANES]
        return [_dot_nt(k[:, kvh * LANES:(kvh + 1) * LANES], qh) for k in k_refs]

    sts_next = scores(0)
    for h in range(heads):
        kvh = h // group
        sts = sts_next
        if h + 1 < heads:
            sts_next = scores(h + 1)
        m = reduce_rows(sts[0], jnp.max)
        for st in sts[1:]:
            m = jnp.maximum(m, reduce_rows(st, jnp.max))
        den = None
        acc = None
        for st, vt in zip(sts, vt_refs):
            p = jnp.exp2(st - m)
            ps = reduce_rows(p, jnp.sum)
            pv = _dot(vt[kvh * MLA_V:(kvh + 1) * MLA_V, :], p.astype(BF16))
            den = ps if den is None else den + ps
            acc = pv if acc is None else acc + pv
        head_outs.append(acc / den)
    pairs = [jnp.concatenate(head_outs[i:i + 2], axis=0).T for i in range(0, heads, 2)]
    o_ref[...] = jnp.concatenate(pairs, axis=-1).astype(BF16)


def _attn(q, ks, vts, batch, heads, group, tq):
    n = q.shape[0]
    t = n // batch
    nq = t // tq
    n_src = len(ks)
    in_specs = [pl.BlockSpec((tq, heads * LANES), lambda b, i: (b * nq + i, 0))]
    for a in ks:
        in_specs.append(pl.BlockSpec((a.shape[0] // batch, a.shape[1]), lambda b, i: (b, 0)))
    for a in vts:
        in_specs.append(pl.BlockSpec((a.shape[0], a.shape[1] // batch), lambda b, i: (0, b)))
    return pl.pallas_call(
        functools.partial(_attn_kernel, n_src=n_src, heads=heads, group=group),
        grid=(batch, nq),
        in_specs=in_specs,
        out_specs=pl.BlockSpec((tq, heads * MLA_V), lambda b, i: (b * nq + i, 0)),
        out_shape=jax.ShapeDtypeStruct((n, heads * MLA_V), BF16),
        compiler_params=_params(("parallel", "parallel")),
        name="attn",
    )(q, *ks, *vts)


def _mxu_transpose(eye, parts):
    out = _dot_nt(eye, parts[0])
    for p in parts[1:]:
        out = out + _dot_nt(eye, p)
    return out


def _eye(n, m):
    return (lax.broadcasted_iota(jnp.int32, (n, m), 0) == lax.broadcasted_iota(jnp.int32, (n, m), 1)).astype(BF16)


def _mlstm_kernel(*refs, chunk, nc, nb, has_init, want_state):
    it = iter(refs)
    fwd = tuple(next(it) for _ in range(6))
    bwd = tuple(next(it) for _ in range(6))
    if has_init:
        c0_ref, n0_ref, m0_ref = (next(it) for _ in range(3))
    h_outs = (next(it), next(it))
    if want_state:
        c_out, n_out, m_out = (next(it) for _ in range(3))
    ct_s, n_s, m_s = (next(it) for _ in range(3))
    step = pl.program_id(1)
    L = chunk
    eye_l = _eye(L, L)
    eye_lanes = _eye(LANES, LANES)
    eye_dk = _eye(ML_DK, ML_DK)
    units = [(j, d, h) for j in range(nb) for d in range(2) for h in range(ML_HEADS)]

    @pl.when(step == 0)
    def _():
        if has_init:
            zpad = jnp.zeros((ML_DK, ML_DK), F32)
            for j, d, h in units:
                c0 = jnp.concatenate([c0_ref[j, d, h], zpad], axis=0)
                ct_s[j, d, h] = _mxu_transpose(eye_dk, _split3(c0))
                n_s[j, d, h] = jnp.concatenate([n0_ref[j, d, h:h + 1, :], zpad[0:1]], axis=1)
            m_s[...] = m0_ref[...]
        else:
            ct_s[...] = jnp.zeros_like(ct_s)
            n_s[...] = jnp.zeros_like(n_s)
            m_s[...] = jnp.zeros_like(m_s)

    prev = {u: (ct_s[u[0], u[1], u[2]], n_s[u[0], u[1], u[2]]) for u in units}
    m_prev = {(j, d): m_s[j, d] for j in range(nb) for d in range(2)}
    new, m_next = {}, {}

    row = lax.broadcasted_iota(jnp.int32, (L, L), 0)
    col = lax.broadcasted_iota(jnp.int32, (L, L), 1)
    sel_row = lax.broadcasted_iota(jnp.int32, (LANES, L), 0)
    head_of_lane = lax.broadcasted_iota(jnp.int32, (L, ML_W), 1) // ML_DK
    ones_rows = jnp.ones((2 * COND_ROWS, L), BF16)
    zero_rows = jnp.zeros((2 * COND_ROWS, L), BF16)
    zero_half = jnp.zeros((ML_DK, L), BF16)
    seq_dirs = [(j, d) for j in range(nb) for d in range(2)]
    blocks = {jd: (fwd if jd[1] == 0 else bwd) for jd in seq_dirs}
    tri = [(col <= row).astype(BF16), (col >= row).astype(BF16)]
    valid_t = [row <= col, row >= col]

    bcum = {(j, d): _dot_exact_lhs(tri[d], jax.nn.log_sigmoid(blocks[(j, d)][5][j])) for j, d in seq_dirs}

    inter_t, w_t, a_parts = {}, {}, {}
    for j, d in seq_dirs:
        a = blocks[(j, d)][4][j] - bcum[(j, d)]
        m_vec = m_prev[(j, d)]
        b_last = bcum[(j, d)][L - 1:L, :] if d == 0 else bcum[(j, d)][0:1, :]
        mx = jnp.maximum(m_vec, jnp.max(a, axis=0, keepdims=True))
        m_next[(j, d)] = b_last + mx
        w_old = jnp.exp(m_vec - mx)
        inter_t[(j, d)] = _mxu_transpose(eye_lanes, _split3(bcum[(j, d)] + m_vec))
        w_t[(j, d)] = _mxu_transpose(eye_lanes, _split3(jnp.exp(a - mx)))
        a_ext = jnp.concatenate([a - m_vec, jnp.broadcast_to(w_old, (2 * COND_ROWS, LANES))], axis=0)
        a_parts[(j, d)] = _split3(a_ext)

    abc, st = {}, {}
    for j, d, h in units:
        pick = (sel_row == d * ML_HEADS + h).astype(BF16)
        parts = a_parts[(j, d)]
        abc[(j, d, h)] = _dot(parts[0], pick) + _dot(parts[1], pick) + _dot(parts[2], pick)
        k_tok = blocks[(j, d)][0][j]
        st[(j, d, h)] = _dot(jnp.where(head_of_lane == h, k_tok, jnp.zeros_like(k_tok)), blocks[(j, d)][1][j])

    upd = {}
    for j, d, h in units:
        u = d * ML_HEADS + h
        kt_h = blocks[(j, d)][2][j, h * ML_DK:(h + 1) * ML_DK, :]
        v_h = blocks[(j, d)][3][j, h * ML_DK:(h + 1) * ML_DK, :]
        kw = (kt_h.astype(F32) * w_t[(j, d)][u:u + 1, :]).astype(BF16)
        upd[(j, d, h)] = _dot_nt(jnp.concatenate([v_h, ones_rows], axis=0),
                                 jnp.concatenate([kw, zero_half], axis=0))

    hts = {}
    for j, d, h in units:
        u = d * ML_HEADS + h
        ct, n_row = prev[(j, d, h)]
        w_old_u = abc[(j, d, h)][L:L + 1, :]
        dm = jnp.where(valid_t[d], abc[(j, d, h)][0:L], -jnp.inf)
        mp = jnp.maximum(jnp.max(dm, axis=0, keepdims=True), 0.0)
        w_inter = jnp.exp(-mp)
        qk_t = st[(j, d, h)] * jnp.exp(dm - mp)
        q_h = blocks[(j, d)][1][j, h * ML_DK:(h + 1) * ML_DK, :]
        v_h = blocks[(j, d)][3][j, h * ML_DK:(h + 1) * ML_DK, :]
        rhs = jnp.concatenate([qk_t.astype(BF16), (q_h.astype(F32) * w_inter).astype(BF16), zero_half], axis=0)
        lhs = jnp.concatenate([
            jnp.concatenate([v_h, ct.astype(BF16)], axis=1),
            jnp.concatenate([zero_rows, jnp.broadcast_to(n_row, (2 * COND_ROWS, LANES)).astype(BF16)], axis=1),
        ], axis=0)
        res = _dot(lhs, rhs)
        den = res[ML_DK:ML_DK + 1] + jnp.sum(qk_t, axis=0, keepdims=True)
        hts[(j, d, h)] = res[0:ML_DK] / jnp.maximum(jnp.abs(den), jnp.exp(-(inter_t[(j, d)][u:u + 1, :] + mp)))
        new[(j, d, h)] = (w_old_u * ct + upd[(j, d, h)][0:ML_DK],
                          w_old_u * n_row + upd[(j, d, h)][ML_DK:ML_DK + 1])

    for j, d in seq_dirs:
        ht_all = jnp.concatenate([hts[(j, d, h)] for h in range(ML_HEADS)], axis=0)
        h_outs[d][j] = _mxu_transpose(eye_l, _split3(ht_all))


    for (j, d, h), (ct_new, n_new) in new.items():
        ct_s[j, d, h] = ct_new
        n_s[j, d, h] = n_new
    for (j, d), m_new in m_next.items():
        m_s[j, d] = m_new

    if want_state:
        @pl.when(step == nc - 1)
        def _():
            pad_eye = _eye(ML_DK, LANES)
            for j, d, h in units:
                c_out[j, d, h] = _mxu_transpose(pad_eye, _split3(new[(j, d, h)][0]))
                n_out[j, d, h:h + 1, :] = new[(j, d, h)][1][:, 0:ML_DK]
            for (j, d), m_new in m_next.items():
                m_out[j, d] = m_new


def _mlstm(mk, mqt, mkt, mvt, gi, gf, batch, chunk, nb, init, l, want_state):
    t = mk.shape[0] // batch
    nc = t // chunk
    seq = lambda a: a.reshape(batch, t, a.shape[-1])
    in_specs, args = [], []
    for cidx in (lambda c: c, lambda c: nc - 1 - c):
        tok = lambda b, c, cidx=cidx: (b, cidx(c), 0)
        tr = lambda b, c, cidx=cidx: (b, 0, cidx(c))
        in_specs += [pl.BlockSpec((nb, chunk, ML_W), tok)] + [pl.BlockSpec((nb, ML_W, chunk), tr)] * 3
        in_specs += [pl.BlockSpec((nb, chunk, LANES), tok)] * 2
        args += [seq(mk), mqt, mkt, mvt, seq(gi), seq(gf)]
    has_init = init is not None
    if has_init:
        in_specs += [
            pl.BlockSpec((nb, None, 2, ML_HEADS, ML_DK, ML_DK), lambda b, c: (b, l, 0, 0, 0, 0)),
            pl.BlockSpec((nb, None, 2, ML_HEADS, ML_DK), lambda b, c: (b, l, 0, 0, 0)),
            pl.BlockSpec((nb, None, 2, 1, LANES), lambda b, c: (b, l, 0, 0, 0)),
        ]
        args += list(init)
    out_specs = [pl.BlockSpec((nb, chunk, ML_W), lambda b, c: (b, c, 0)),
                 pl.BlockSpec((nb, chunk, ML_W), lambda b, c: (b, nc - 1 - c, 0))]
    out_shape = [jax.ShapeDtypeStruct((batch, t, ML_W), F32)] * 2
    if want_state:
        out_specs += [
            pl.BlockSpec((nb, 2, ML_HEADS, ML_DK, ML_DK), lambda b, c: (b, 0, 0, 0, 0)),
            pl.BlockSpec((nb, 2, ML_HEADS, ML_DK), lambda b, c: (b, 0, 0, 0)),
            pl.BlockSpec((nb, 2, 1, LANES), lambda b, c: (b, 0, 0, 0)),
        ]
        out_shape += [jax.ShapeDtypeStruct((batch, 2, ML_HEADS, ML_DK, ML_DK), F32),
                      jax.ShapeDtypeStruct((batch, 2, ML_HEADS, ML_DK), F32),
                      jax.ShapeDtypeStruct((batch, 2, 1, LANES), F32)]
    outs = pl.pallas_call(
        functools.partial(_mlstm_kernel, chunk=chunk, nc=nc, nb=nb, has_init=has_init, want_state=want_state),
        grid=(batch // nb, nc),
        in_specs=in_specs,
        out_specs=out_specs,
        out_shape=out_shape,
        scratch_shapes=[pltpu.VMEM((nb, 2, ML_HEADS, ML_DK, LANES), F32), pltpu.VMEM((nb, 2, ML_HEADS, 1, LANES), F32),
                        pltpu.VMEM((nb, 2, 1, LANES), F32)],
        compiler_params=_params(("parallel", "arbitrary")),
        name="mlstm",
    )(*args)
    return [outs[0].reshape(batch * t, ML_W), outs[1].reshape(batch * t, ML_W)] + list(outs[2:])


def _outproj_kernel(x_ref, mod_ref, go_ref, hf_ref, hb_ref, mo_ref, lo_ref, onw_ref, seg_ref, w_ref, o_ref):
    gqa_w = GQA_HEADS * HEAD_DIM
    hsum = hf_ref[...] + hb_ref[...]
    hi, mid, lo = _split3(hsum * hsum)
    seg = seg_ref[...]
    ms = _dot(hi, seg) + _dot(mid, seg) + _dot(lo, seg)
    hn = hsum * lax.rsqrt(ms + EPS) * onw_ref[...]
    ml = (hn * jax.nn.sigmoid(mo_ref[...])).astype(BF16)
    out = (_dot(go_ref[...], w_ref[0:gqa_w, :]) + _dot(lo_ref[...], w_ref[gqa_w + ML_W:, :])
           + _dot(ml, w_ref[gqa_w:gqa_w + ML_W, :]))
    o_ref[...] = x_ref[...] + mod_ref[5:6, :] * out


def _outproj(x, mods, gqa_o, hf, hb, mo, mla_o, onw, seg, w_out, l, group_of_tile, tm):
    n, d = x.shape
    row = lambda i: (i, 0)
    lay = lambda shape: pl.BlockSpec((None,) + shape, lambda i: (l,) + (0,) * len(shape))
    return pl.pallas_call(
        _outproj_kernel,
        grid=(n // tm,),
        in_specs=[
            pl.BlockSpec((tm, d), row),
            pl.BlockSpec((None, None, N_MOD, d), lambda i: (l, group_of_tile(i), 0, 0)),
            pl.BlockSpec((tm, GQA_HEADS * HEAD_DIM), row),
            pl.BlockSpec((tm, ML_W), row), pl.BlockSpec((tm, ML_W), row), pl.BlockSpec((tm, ML_W), row),
            pl.BlockSpec((tm, MLA_HEADS * MLA_V), row),
            lay((1, ML_W)),
            pl.BlockSpec((ML_W, ML_W), lambda i: (0, 0)),
            lay((w_out.shape[1], d)),
        ],
        out_specs=pl.BlockSpec((tm, d), row),
        out_shape=jax.ShapeDtypeStruct((n, d), F32),
        compiler_params=_params(("parallel",)),
        name="outproj",
    )(x, mods, gqa_o, hf, hb, mo, mla_o, onw, seg, w_out)


def _final_norm_kernel(x_ref, w_ref, o_ref):
    x = x_ref[...]
    ms = jnp.mean(x * x, axis=-1, keepdims=True)
    o_ref[...] = x * lax.rsqrt(ms + EPS) * w_ref[...]


def _final_norm(x, w, tm):
    n, d = x.shape
    return pl.pallas_call(
        _final_norm_kernel,
        grid=(n // tm,),
        in_specs=[pl.BlockSpec((tm, d), lambda i: (i, 0)), pl.BlockSpec((1, d), lambda i: (0, 0))],
        out_specs=pl.BlockSpec((tm, d), lambda i: (i, 0)),
        out_shape=jax.ShapeDtypeStruct((n, d), F32),
        compiler_params=_params(("parallel",)),
        name="final_norm",
    )(x, w.reshape(1, d))


def _pad_last(a, width):
    return jnp.pad(a, [(0, 0)] * (a.ndim - 1) + [(0, width - a.shape[-1])])


def _pad_heads(a, heads, width):
    lead = a.shape[:-1]
    return _pad_last(a.reshape(lead + (heads, width)), LANES).reshape(lead + (heads * LANES,))


def _axial_rope(seq, rot_dim):
    half = rot_dim // 2
    rows = seq // GRID_W
    freqs = ROPE_BASE ** (-jnp.arange(0, half, 2, dtype=F32) / half)
    r = jnp.repeat(jnp.arange(rows, dtype=F32), GRID_W)
    c = jnp.tile(jnp.arange(GRID_W, dtype=F32), rows)
    ang = jnp.concatenate([r[:, None] * freqs, c[:, None] * freqs], axis=-1)
    return jnp.cos(ang), jnp.sin(ang)


def _rope_tables(seq):
    c, s = _axial_rope(seq, HEAD_DIM)
    ones = jnp.ones((seq, LANES - HEAD_DIM), F32)
    cq = jnp.concatenate([c, c, ones], axis=-1)
    sq = jnp.concatenate([-s, s, 0.0 * ones], axis=-1)
    c, s = _axial_rope(seq, MLA_ROPE)
    one_a = jnp.ones((seq, MLA_NOPE), F32)
    one_b = jnp.ones((seq, LANES - MLA_QK), F32)
    cm = jnp.concatenate([one_a, c, c, one_b], axis=-1)
    sm = jnp.concatenate([0.0 * one_a, -s, s, 0.0 * one_b], axis=-1)
    return cq, sq, cm, sm


def _pick_tile(n, pref):
    t = min(n, pref)
    while n % t:
        t //= 2
    return t


def kernel(x_prompt, x_sample, c, cache_gqa_k, cache_gqa_v, cache_mla_ckv, cache_mla_krope, state_mlstm_C, state_mlstm_n, state_mlstm_m, c_ctx, w_ada, b_ada, norm_w, ffn_w_gate, ffn_w_up, ffn_w_down, w_in, gqa_q_norm, gqa_k_norm, mlstm_gate_b, mlstm_out_norm, mla_q_norm, mla_w_uq, mla_kv_norm, mla_w_ukv, w_out, final_norm):
    bc, tc, d = x_prompt.shape
    bl, tl, _ = x_sample.shape
    depth = w_ada.shape[0]
    past = cache_gqa_k.shape[2]
    assert 1 + bl <= COND_ROWS

    cond = jnp.concatenate([c_ctx[None, :], c, jnp.zeros((COND_ROWS - 1 - bl, d), F32)], axis=0)
    mods = _ada(cond, w_ada, b_ada).reshape(depth, COND_ROWS, N_MOD, d)

    wg, wu, wd = ffn_w_gate.astype(BF16), ffn_w_up.astype(BF16), ffn_w_down.astype(BF16)
    idx = [0]
    for wdt in (GQA_HEADS * HEAD_DIM, GQA_KV_HEADS * HEAD_DIM, GQA_KV_HEADS * HEAD_DIM, ML_W, ML_W, ML_W, ML_W,
                4 * ML_HEADS, MLA_RANK, MLA_RANK, MLA_ROPE):
        idx.append(idx[-1] + wdt)
    seg = [w_in[..., a:b] for a, b in zip(idx[:-1], idx[1:])]
    s_gq, s_gk, s_gv, s_mq, s_mk, s_mv, s_mo, s_mg, s_ql, s_kv, s_kr = seg
    kr_slot_w = jnp.pad(s_kr, [(0, 0), (0, 0), (MLA_NOPE, LANES - MLA_QK)])
    nh = ML_HEADS
    gi_slot_w = _pad_last(jnp.concatenate([s_mg[..., 0:nh], s_mg[..., 2 * nh:3 * nh]], axis=-1), LANES)
    gf_slot_w = _pad_last(jnp.concatenate([s_mg[..., nh:2 * nh], s_mg[..., 3 * nh:4 * nh]], axis=-1), LANES)
    w_in_p = jnp.concatenate(
        [_pad_heads(s_gq, GQA_HEADS, HEAD_DIM), _pad_heads(s_gk, GQA_KV_HEADS, HEAD_DIM), s_gv,
         s_mk, s_mo, s_ql, s_kv, kr_slot_w, gi_slot_w, gf_slot_w], axis=-1).astype(BF16)
    assert w_in_p.shape[-1] == D_INP
    wmt = jnp.swapaxes(jnp.concatenate([s_mq, s_mk, s_mv], axis=-1), 1, 2).astype(BF16)
    wq = _pad_heads(mla_w_uq, MLA_HEADS, MLA_QK).astype(BF16)
    ukv = mla_w_ukv.reshape(depth, MLA_RANK, MLA_HEADS, MLA_NOPE + MLA_V)
    wk = _pad_last(ukv[..., :MLA_NOPE], LANES).reshape(depth, MLA_RANK, MLA_HEADS * LANES).astype(BF16)
    wvt = jnp.swapaxes(ukv[..., MLA_NOPE:].reshape(depth, MLA_RANK, MLA_HEADS * MLA_V), 1, 2).astype(BF16)
    wgvt = jnp.swapaxes(s_gv, 1, 2).astype(BF16)
    w_out_b = w_out.astype(BF16)
    qn = _pad_last(gqa_q_norm, LANES).reshape(depth, 1, LANES)
    kn = _pad_last(gqa_k_norm, LANES).reshape(depth, 1, LANES)
    gbi = _pad_last(jnp.concatenate([mlstm_gate_b[:, 0], mlstm_gate_b[:, 2]], axis=-1), LANES).reshape(depth, 1, LANES)
    gbf = _pad_last(jnp.concatenate([mlstm_gate_b[:, 1], mlstm_gate_b[:, 3]], axis=-1), LANES).reshape(depth, 1, LANES)
    mqn = mla_q_norm.reshape(depth, 1, MLA_RANK)
    mkvn = mla_kv_norm.reshape(depth, 1, MLA_RANK)
    onw = mlstm_out_norm.reshape(depth, 1, ML_W)
    norm_w4 = norm_w.reshape(depth, 3, 1, d)
    head_id = jnp.arange(ML_W) // ML_DK
    seg_mean = ((head_id[:, None] == head_id[None, :]).astype(F32) / ML_DK).astype(BF16)
    rope_tabs = _rope_tables(tl)

    ck_gqa = _pad_heads(cache_gqa_k.reshape(bl, depth, past, GQA_KV_HEADS * HEAD_DIM), GQA_KV_HEADS,
                        HEAD_DIM).astype(BF16)
    cvt_gqa = jnp.transpose(cache_gqa_v.reshape(bl * depth, past, GQA_KV_HEADS * HEAD_DIM), (2, 0, 1)).reshape(
        GQA_KV_HEADS * HEAD_DIM, bl, depth, past).astype(BF16)
    ckr_slot = jnp.pad(cache_mla_krope, [(0, 0), (0, 0), (0, 0), (MLA_NOPE, LANES - MLA_QK)])
    m0_lanes = jnp.stack([jnp.pad(state_mlstm_m[:, :, dd], [(0, 0), (0, 0), (dd * nh, LANES - (dd + 1) * nh)])
                          for dd in range(2)], axis=2)[:, :, :, None, :]

    nc_rows, nl_rows = bc * tc, bl * tl
    tm_c, tm_l = _pick_tile(nc_rows, 512), _pick_tile(tl, 512)
    tm_ci = _pick_tile(tc, 512)
    tq_c, tq_l = _pick_tile(tc, 512), _pick_tile(tl, 512)
    chunk_c = chunk_l = LANES
    assert tc % LANES == 0 and tl % LANES == 0
    nb_c, nb_l = _pick_tile(bc, 4), _pick_tile(bl, 2)
    ctx_group = lambda i: 0
    lat_group = lambda i: 1 + i // (tl // tm_l)

    xc = x_prompt.reshape(nc_rows, d)
    xl = x_sample.reshape(nl_rows, d)
    collected = [[] for _ in range(7)]
    for l in range(depth):
        xc = _ffn(xc, mods, norm_w4, wg, wu, wd, l, 0, ctx_group, tm_c)
        xl = _ffn(xl, mods, norm_w4, wg, wu, wd, l, 0, lat_group, tm_l)

        (gq, gk, gv, mk, mqt, mkt, mvt, mo, gi, gf, qm, kp, vm, gkf, gvf, ckv, krs) = _inproj(
            xc, mods, norm_w4, w_in_p, wgvt, wmt, qn, kn, gbi, gbf, mqn, mkvn, wq, wk, wvt, None, l, ctx_group, tm_ci,
            tc, True)
        gqa_o = _attn(gq, [gk], [gv], bc, GQA_HEADS, GQA_GROUP, tq_c)
        mla_o = _attn(qm, [kp], [vm], bc, MLA_HEADS, 1, tq_c)
        hf, hb, c_new, n_new, m_lanes = _mlstm(mk, mqt, mkt, mvt, gi, gf, bc, chunk_c, nb_c, None, l, True)
        m_new = jnp.stack([m_lanes[:, dd, 0, dd * nh:(dd + 1) * nh] for dd in range(2)], axis=1)
        xc = _outproj(xc, mods, gqa_o, hf, hb, mo, mla_o, onw, seg_mean, w_out_b, l, ctx_group, tm_c)
        new_k = gkf.reshape(bc, tc, GQA_KV_HEADS, LANES)[..., :HEAD_DIM]
        new_v = gvf.reshape(bc, tc, GQA_KV_HEADS, HEAD_DIM)
        new_ckv = ckv.reshape(bc, tc, MLA_RANK)
        new_kr = krs.reshape(bc, tc, LANES)[..., MLA_NOPE:MLA_QK]
        for lst, t in zip(collected, (new_k, new_v, new_ckv, new_kr, c_new, n_new, m_new)):
            lst.append(t)

        (gq, gk, gv, mk, mqt, mkt, mvt, mo, gi, gf, qm, kp, vm) = _inproj(
            xl, mods, norm_w4, w_in_p, wgvt, wmt, qn, kn, gbi, gbf, mqn, mkvn, wq, wk, wvt, rope_tabs, l, lat_group,
            tm_l, tl, False)
        kp_c, vm_c = _kvexp(cache_mla_ckv[:, l].reshape(bl * past, MLA_RANK), ckr_slot[:, l].reshape(bl * past, LANES),
                            wk, wvt, l)
        gqa_o = _attn(gq, [ck_gqa[:, l].reshape(bl * past, -1), gk], [cvt_gqa[:, :, l].reshape(-1, bl * past), gv],
                      bl, GQA_HEADS, GQA_GROUP, tq_l)
        mla_o = _attn(qm, [kp_c, kp], [vm_c, vm], bl, MLA_HEADS, 1, tq_l)
        hf, hb = _mlstm(mk, mqt, mkt, mvt, gi, gf, bl, chunk_l, nb_l, (state_mlstm_C, state_mlstm_n, m0_lanes), l,
                        False)
        xl = _outproj(xl, mods, gqa_o, hf, hb, mo, mla_o, onw, seg_mean, w_out_b, l, lat_group, tm_l)

        xc = _ffn(xc, mods, norm_w4, wg, wu, wd, l, 1, ctx_group, tm_c)
        xl = _ffn(xl, mods, norm_w4, wg, wu, wd, l, 1, lat_group, tm_l)

    y_prompt = _final_norm(xc, final_norm, tm_c).reshape(bc, tc, d)
    y_sample = _final_norm(xl, final_norm, tm_l).reshape(bl, tl, d)
    stacked = [jnp.stack(lst, axis=1) for lst in collected]
    return (y_prompt, y_sample, *stacked)
```

```python
import functools

import jax
import jax.numpy as jnp
from jax import lax
from jax.experimental import pallas as pl
from jax.experimental.pallas import tpu as pltpu

F32 = jnp.float32
BF16 = jnp.bfloat16

EPS = 1e-6
ROPE_BASE = 10000.0
GRID_W = 64
N_MOD = 9
HEAD_DIM = 64
GQA_HEADS = 6
GQA_KV_HEADS = 2
GQA_GROUP = GQA_HEADS // GQA_KV_HEADS
ML_HEADS = 4
ML_DK = 64
ML_W = ML_HEADS * ML_DK
MLA_HEADS = 6
MLA_RANK = 256
MLA_NOPE = 64
MLA_ROPE = 32
MLA_V = 64
MLA_QK = MLA_NOPE + MLA_ROPE
LANES = 128
COND_ROWS = 8
VMEM_LIMIT = 48 * 1024 * 1024
LOG2E = 1.4426950408889634
REDUCE_ROWS = 64

O_GQ = 0
O_GK = O_GQ + GQA_HEADS * LANES
O_GV = O_GK + GQA_KV_HEADS * LANES
O_MK = O_GV + GQA_KV_HEADS * HEAD_DIM
O_MO = O_MK + ML_W
O_QL = O_MO + ML_W
O_KV = O_QL + MLA_RANK
O_KR = O_KV + MLA_RANK
O_GI = O_KR + LANES
O_GF = O_GI + LANES
D_INP = O_GF + LANES


def _params(sem):
    return pltpu.CompilerParams(dimension_semantics=sem, vmem_limit_bytes=VMEM_LIMIT)


def _norm_mod(x, nw, sc, sh):
    ms = jnp.mean(x * x, axis=-1, keepdims=True)
    return (x * lax.rsqrt(ms + EPS) * nw) * (1.0 + sc) + sh


def _dot(a, b):
    return jnp.dot(a, b, preferred_element_type=F32)


def _dot_nt(a, b):
    return lax.dot_general(a, b, (((1,), (1,)), ((), ())), preferred_element_type=F32)


def _dot_tn(a, b):
    return lax.dot_general(a, b, (((0,), (0,)), ((), ())), preferred_element_type=F32)


def _split2(x):
    hi = x.astype(BF16)
    return hi, (x - hi.astype(F32)).astype(BF16)


def _split3(x):
    hi = x.astype(BF16)
    r1 = x - hi.astype(F32)
    mid = r1.astype(BF16)
    lo = (r1 - mid.astype(F32)).astype(BF16)
    return hi, mid, lo


def _dot_exact_lhs(a_bf16, x):
    hi, mid, lo = _split3(x)
    return _dot(a_bf16, hi) + _dot(a_bf16, mid) + _dot(a_bf16, lo)


def _ada_kernel(cond_ref, w_ref, b_ref, o_ref):
    c = cond_ref[...]
    s = (c * jax.nn.sigmoid(c)).astype(BF16)
    o_ref[...] = _dot(s, w_ref[...].astype(BF16)) + b_ref[...]


def _ada(cond, w_ada, b_ada):
    depth, d, nd = w_ada.shape
    tn = d
    return pl.pallas_call(
        _ada_kernel,
        grid=(depth, nd // tn),
        in_specs=[
            pl.BlockSpec((COND_ROWS, d), lambda l, j: (0, 0)),
            pl.BlockSpec((None, d, tn), lambda l, j: (l, 0, j)),
            pl.BlockSpec((None, 1, tn), lambda l, j: (l, 0, j)),
        ],
        out_specs=pl.BlockSpec((None, COND_ROWS, tn), lambda l, j: (l, 0, j)),
        out_shape=jax.ShapeDtypeStruct((depth, COND_ROWS, nd), F32),
        compiler_params=_params(("parallel", "parallel")),
        name="ada",
    )(cond, w_ada, b_ada.reshape(depth, 1, nd))


def _ffn_kernel(x_ref, mod_ref, nw_ref, wg_ref, wu_ref, wd_ref, o_ref, *, mi):
    x = x_ref[...]
    h = _norm_mod(x, nw_ref[...], mod_ref[mi + 1:mi + 2, :], mod_ref[mi:mi + 1, :]).astype(BF16)
    g = _dot(h, wg_ref[...])
    u = _dot(h, wu_ref[...])
    a = (g * jax.nn.sigmoid(g) * u).astype(BF16)
    o_ref[...] = x + (0.5 * mod_ref[mi + 2:mi + 3, :]) * _dot(a, wd_ref[...])


def _ffn(x, mods, norm_w4, wg, wu, wd, l, j, group_of_tile, tm):
    n, d = x.shape
    dff = wg.shape[-1]
    mi = 0 if j == 0 else 6
    nwi = 0 if j == 0 else 2
    resident = lambda shape: pl.BlockSpec((None, None) + shape, lambda i: (l, j, 0, 0),
                                          pipeline_mode=pl.Buffered(1))
    return pl.pallas_call(
        functools.partial(_ffn_kernel, mi=mi),
        grid=(n // tm,),
        in_specs=[
            pl.BlockSpec((tm, d), lambda i: (i, 0)),
            pl.BlockSpec((None, None, N_MOD, d), lambda i: (l, group_of_tile(i), 0, 0)),
            pl.BlockSpec((None, None, 1, d), lambda i: (l, nwi, 0, 0)),
            resident((d, dff)), resident((d, dff)), resident((dff, d)),
        ],
        out_specs=pl.BlockSpec((tm, d), lambda i: (i, 0)),
        out_shape=jax.ShapeDtypeStruct((n, d), F32),
        compiler_params=_params(("parallel",)),
        name="ffn",
    )(x, mods, norm_w4, wg, wu, wd)


def _inproj_kernel(*refs, rope, ctx_out):
    it = iter(refs)
    (x_ref, mod_ref, nw_ref, w_ref, wgvt_ref, wmt_ref, qn_ref, kn_ref, gbi_ref, gbf_ref, mqn_ref, mkvn_ref, wq_ref,
     wk_ref, wvt_ref) = (next(it) for _ in range(15))
    if rope:
        cq_ref, sq_ref, cm_ref, sm_ref = (next(it) for _ in range(4))
    gq_o, gk_o, gv_o, mk_o, mqt_o, mkt_o, mvt_o, mo_o, gi_o, gf_o, qm_o, kp_o, vm_o = (next(it) for _ in range(13))
    if ctx_out:
        gkf_o, gvf_o, ckv_o, kr_o = (next(it) for _ in range(4))

    x = x_ref[...]
    tm = x.shape[0]
    hb = _norm_mod(x, nw_ref[...], mod_ref[4:5, :], mod_ref[3:4, :]).astype(BF16)

    def proj(o, n):
        return _dot(hb, w_ref[:, o:o + n])

    lane = lax.broadcasted_iota(jnp.int32, (tm, LANES), 1)

    def head_norm(xg, wrow):
        ms = jnp.sum(xg * xg, axis=-1, keepdims=True) * (1.0 / HEAD_DIM)
        return xg * lax.rsqrt(ms + EPS) * wrow

    def rope_hd(xg):
        half = HEAD_DIM // 2
        partner = jnp.where(lane < half, pltpu.roll(xg, LANES - half, 1), pltpu.roll(xg, half, 1))
        return xg * cq_ref[...] + partner * sq_ref[...]

    def rope_mla(xg):
        half = MLA_ROPE // 2
        partner = jnp.where((lane & half) == 0, pltpu.roll(xg, LANES - half, 1), pltpu.roll(xg, half, 1))
        return xg * cm_ref[...] + partner * sm_ref[...]

    gq = proj(O_GQ, GQA_HEADS * LANES)
    for g in range(GQA_HEADS):
        qg = head_norm(gq[:, g * LANES:(g + 1) * LANES], qn_ref[...])
        if rope:
            qg = rope_hd(qg)
        gq_o[:, g * LANES:(g + 1) * LANES] = (qg * (LOG2E * HEAD_DIM ** -0.5)).astype(BF16)
    gk = proj(O_GK, GQA_KV_HEADS * LANES)
    for g in range(GQA_KV_HEADS):
        kg = head_norm(gk[:, g * LANES:(g + 1) * LANES], kn_ref[...])
        if ctx_out:
            gkf_o[:, g * LANES:(g + 1) * LANES] = kg
        if rope:
            kg = rope_hd(kg)
        gk_o[:, g * LANES:(g + 1) * LANES] = kg.astype(BF16)
    gv_o[...] = _dot_nt(wgvt_ref[...], hb).astype(BF16)
    if ctx_out:
        gvf_o[...] = proj(O_GV, GQA_KV_HEADS * HEAD_DIM)

    mk_o[...] = proj(O_MK, ML_W).astype(BF16)
    mt = _dot_nt(wmt_ref[...], hb)
    rs = mqt_o.shape[2]
    for s in range(mqt_o.shape[0]):
        cols = slice(s * rs, (s + 1) * rs)
        mqt_o[s] = (mt[0:ML_W, cols] * (ML_DK ** -0.5)).astype(BF16)
        mkt_o[s] = mt[ML_W:2 * ML_W, cols].astype(BF16)
        mvt_o[s] = mt[2 * ML_W:3 * ML_W, cols].astype(BF16)
    mo_o[...] = proj(O_MO, ML_W)
    gi_o[...] = proj(O_GI, LANES) + gbi_ref[...]
    gf_o[...] = proj(O_GF, LANES) + gbf_ref[...]

    xq = proj(O_QL, MLA_RANK)
    ms = jnp.mean(xq * xq, axis=-1, keepdims=True)
    qn = (xq * lax.rsqrt(ms + EPS) * mqn_ref[...]).astype(BF16)
    qm = _dot(qn, wq_ref[...])
    for g in range(MLA_HEADS):
        qg = qm[:, g * LANES:(g + 1) * LANES]
        if rope:
            qg = rope_mla(qg)
        qm_o[:, g * LANES:(g + 1) * LANES] = (qg * (LOG2E * MLA_QK ** -0.5)).astype(BF16)

    xkv = proj(O_KV, MLA_RANK)
    ms = jnp.mean(xkv * xkv, axis=-1, keepdims=True)
    ckv = xkv * lax.rsqrt(ms + EPS) * mkvn_ref[...]
    kr = proj(O_KR, LANES)
    if rope:
        kr = rope_mla(kr)
    if ctx_out:
        ckv_o[...] = ckv
        kr_o[...] = kr
    cb = ckv.astype(BF16)
    kk = _dot(cb, wk_ref[...])
    for g in range(MLA_HEADS):
        kp_o[:, g * LANES:(g + 1) * LANES] = (kk[:, g * LANES:(g + 1) * LANES] + kr).astype(BF16)
    vm_o[...] = _dot_nt(wvt_ref[...], cb).astype(BF16)


def _inproj(x, mods, norm_w4, w_in_p, wgvt, wmt, qn, kn, gbi, gbf, mqn, mkvn, wq, wk, wvt, rope_tabs, l, group_of_tile,
            tm, rows_per_seq, ctx_out):
    n, d = x.shape
    rope = rope_tabs is not None
    tiles_per_seq = max(1, rows_per_seq // tm)
    seqs_per_tile = max(1, tm // rows_per_seq)
    rows_in_tile = tm // seqs_per_tile
    row = lambda i: (i, 0)
    lay = lambda shape: pl.BlockSpec((None,) + shape, lambda i: (l,) + (0,) * len(shape))
    in_specs = [
        pl.BlockSpec((tm, d), row),
        pl.BlockSpec((None, None, N_MOD, d), lambda i: (l, group_of_tile(i), 0, 0)),
        pl.BlockSpec((None, None, 1, d), lambda i: (l, 1, 0, 0)),
        lay((d, D_INP)), lay((GQA_KV_HEADS * HEAD_DIM, d)), lay((3 * ML_W, d)),
        lay((1, LANES)), lay((1, LANES)), lay((1, LANES)), lay((1, LANES)), lay((1, MLA_RANK)), lay((1, MLA_RANK)),
        lay((MLA_RANK, MLA_HEADS * LANES)), lay((MLA_RANK, MLA_HEADS * LANES)), lay((MLA_HEADS * MLA_V, MLA_RANK)),
    ]
    args = [x, mods, norm_w4, w_in_p, wgvt, wmt, qn, kn, gbi, gbf, mqn, mkvn, wq, wk, wvt]
    if rope:
        tab = pl.BlockSpec((tm, LANES), lambda i: (i % tiles_per_seq, 0))
        in_specs += [tab] * 4
        args += list(rope_tabs)
    widths = [(GQA_HEADS * LANES, BF16, "row"), (GQA_KV_HEADS * LANES, BF16, "row"),
              (GQA_KV_HEADS * HEAD_DIM, BF16, "col"),
              (ML_W, BF16, "row"), (ML_W, BF16, "seq"), (ML_W, BF16, "seq"), (ML_W, BF16, "seq"),
              (ML_W, F32, "row"), (LANES, F32, "row"), (LANES, F32, "row"),
              (MLA_HEADS * LANES, BF16, "row"), (MLA_HEADS * LANES, BF16, "row"), (MLA_HEADS * MLA_V, BF16, "col")]
    if ctx_out:
        widths += [(GQA_KV_HEADS * LANES, F32, "row"), (GQA_KV_HEADS * HEAD_DIM, F32, "row"),
                   (MLA_RANK, F32, "row"), (LANES, F32, "row")]
    spec = {"row": lambda w: pl.BlockSpec((tm, w), row),
            "col": lambda w: pl.BlockSpec((w, tm), lambda i: (0, i)),
            "seq": lambda w: pl.BlockSpec((seqs_per_tile, w, rows_in_tile),
                                          lambda i: (i // tiles_per_seq, 0, i % tiles_per_seq))}
    shape = {"row": lambda w: (n, w), "col": lambda w: (w, n), "seq": lambda w: (n // rows_per_seq, w, rows_per_seq)}
    return pl.pallas_call(
        functools.partial(_inproj_kernel, rope=rope, ctx_out=ctx_out),
        grid=(n // tm,),
        in_specs=in_specs,
        out_specs=[spec[kind](w) for w, _, kind in widths],
        out_shape=[jax.ShapeDtypeStruct(shape[kind](w), dt) for w, dt, kind in widths],
        compiler_params=_params(("parallel",)),
        name="inproj",
    )(*args)


def _kvexp_kernel(ckv_ref, kr_ref, wk_ref, wvt_ref, kp_o, vm_o):
    cb = ckv_ref[...].astype(BF16)
    kk = _dot(cb, wk_ref[...])
    kr = kr_ref[...]
    for g in range(MLA_HEADS):
        kp_o[:, g * LANES:(g + 1) * LANES] = (kk[:, g * LANES:(g + 1) * LANES] + kr).astype(BF16)
    vm_o[...] = _dot_nt(wvt_ref[...], cb).astype(BF16)


def _kvexp(ckv, kr_slot, wk, wvt, l):
    n = ckv.shape[0]
    lay = lambda shape: pl.BlockSpec((None,) + shape, lambda i: (l,) + (0,) * len(shape))
    return pl.pallas_call(
        _kvexp_kernel,
        grid=(1,),
        in_specs=[pl.BlockSpec((n, MLA_RANK), lambda i: (0, 0)), pl.BlockSpec((n, LANES), lambda i: (0, 0)),
                  lay((MLA_RANK, MLA_HEADS * LANES)), lay((MLA_HEADS * MLA_V, MLA_RANK))],
        out_specs=[pl.BlockSpec((n, MLA_HEADS * LANES), lambda i: (0, 0)),
                   pl.BlockSpec((MLA_HEADS * MLA_V, n), lambda i: (0, 0))],
        out_shape=[jax.ShapeDtypeStruct((n, MLA_HEADS * LANES), BF16),
                   jax.ShapeDtypeStruct((MLA_HEADS * MLA_V, n), BF16)],
        compiler_params=_params(("arbitrary",)),
        name="kvexp",
    )(ckv, kr_slot, wk, wvt)


def _attn_kernel(*refs, n_src, heads, group, nseq, tq):
    q_ref = refs[0]
    k_refs = refs[1:1 + n_src]
    vt_refs = refs[1 + n_src:1 + 2 * n_src]
    o_ref = refs[1 + 2 * n_src]
    seq_rows = [k.shape[0] // nseq for k in k_refs]

    def max_rows(x):
        r = x.shape[0]
        part = jnp.max(x.reshape(r // REDUCE_ROWS, REDUCE_ROWS, x.shape[1]), axis=0) if r > REDUCE_ROWS else x
        return jnp.max(part, axis=0, keepdims=True)

    def scores(j, h):
        kvh = h // group
        qh = q_ref[j * tq:(j + 1) * tq, h * LANES:(h + 1) * LANES]
        return [_dot_nt(k[j * s:(j + 1) * s, kvh * LANES:(kvh + 1) * LANES], qh)
                for k, s in zip(k_refs, seq_rows)]

    units = [(j, h) for j in range(nseq) for h in range(heads)]
    sts_next = scores(*units[0])
    head_outs = []
    for idx, (j, h) in enumerate(units):
        kvh = h // group
        sts = sts_next
        if idx + 1 < len(units):
            sts_next = scores(*units[idx + 1])
        m = max_rows(sts[0])
        for st in sts[1:]:
            m = jnp.maximum(m, max_rows(st))
        acc = None
        for st, vt, s in zip(sts, vt_refs, seq_rows):
            p = jnp.exp2(st - m).astype(BF16)
            v_ext = jnp.concatenate([vt[kvh * MLA_V:(kvh + 1) * MLA_V, j * s:(j + 1) * s],
                                     jnp.ones((2 * COND_ROWS, s), BF16)], axis=0)
            pv = _dot(v_ext, p)
            acc = pv if acc is None else acc + pv
        head_outs.append(acc[0:MLA_V] / acc[MLA_V:MLA_V + 1])
        if h == heads - 1:
            pairs = [jnp.concatenate(head_outs[i:i + 2], axis=0).T for i in range(0, heads, 2)]
            o_ref[j * tq:(j + 1) * tq, :] = jnp.concatenate(pairs, axis=-1).astype(BF16)
            head_outs = []


def _attn(q, ks, vts, batch, heads, group, tq, nseq):
    n = q.shape[0]
    t = n // batch
    nq = t // tq
    assert nseq == 1 or nq == 1
    n_src = len(ks)
    in_specs = [pl.BlockSpec((nseq * tq, heads * LANES), lambda b, i: (b * nq + i, 0))]
    for a in ks:
        in_specs.append(pl.BlockSpec((nseq * a.shape[0] // batch, a.shape[1]), lambda b, i: (b, 0)))
    for a in vts:
        in_specs.append(pl.BlockSpec((a.shape[0], nseq * a.shape[1] // batch), lambda b, i: (0, b)))
    return pl.pallas_call(
        functools.partial(_attn_kernel, n_src=n_src, heads=heads, group=group, nseq=nseq, tq=tq),
        grid=(batch // nseq, nq),
        in_specs=in_specs,
        out_specs=pl.BlockSpec((nseq * tq, heads * MLA_V), lambda b, i: (b * nq + i, 0)),
        out_shape=jax.ShapeDtypeStruct((n, heads * MLA_V), BF16),
        compiler_params=_params(("parallel", "parallel")),
        name="attn",
    )(q, *ks, *vts)


def _mxu_transpose(eye, parts):
    out = _dot_nt(eye, parts[0])
    for p in parts[1:]:
        out = out + _dot_nt(eye, p)
    return out


def _eye(n, m):
    return (lax.broadcasted_iota(jnp.int32, (n, m), 0) == lax.broadcasted_iota(jnp.int32, (n, m), 1)).astype(BF16)


def _mlstm_kernel(*refs, chunk, nc, nb, has_init, want_state):
    it = iter(refs)
    fwd = tuple(next(it) for _ in range(6))
    bwd = tuple(next(it) for _ in range(6))
    if has_init:
        c0_ref, n0_ref, m0_ref = (next(it) for _ in range(3))
    h_outs = (next(it), next(it))
    if want_state:
        c_out, n_out, m_out = (next(it) for _ in range(3))
    ct_s, n_s, m_s = (next(it) for _ in range(3))
    step = pl.program_id(1)
    L = chunk
    eye_l = _eye(L, L)
    eye_lanes = _eye(LANES, LANES)
    eye_dk = _eye(ML_DK, ML_DK)
    units = [(j, d, h) for j in range(nb) for d in range(2) for h in range(ML_HEADS)]

    @pl.when(step == 0)
    def _():
        if has_init:
            zpad = jnp.zeros((ML_DK, ML_DK), F32)
            for j, d, h in units:
                c0 = jnp.concatenate([c0_ref[j, d, h], zpad], axis=0)
                ct_s[j, d, h] = _mxu_transpose(eye_dk, _split3(c0))
                n_s[j, d, h] = jnp.concatenate([n0_ref[j, d, h:h + 1, :], zpad[0:1]], axis=1)
            m_s[...] = m0_ref[...]
        else:
            ct_s[...] = jnp.zeros_like(ct_s)
            n_s[...] = jnp.zeros_like(n_s)
            m_s[...] = jnp.zeros_like(m_s)

    prev = {u: (ct_s[u[0], u[1], u[2]], n_s[u[0], u[1], u[2]]) for u in units}
    m_prev = {(j, d): m_s[j, d] for j in range(nb) for d in range(2)}
    new, m_next = {}, {}

    row = lax.broadcasted_iota(jnp.int32, (L, L), 0)
    col = lax.broadcasted_iota(jnp.int32, (L, L), 1)
    sel_row = lax.broadcasted_iota(jnp.int32, (LANES, L), 0)
    head_of_lane = lax.broadcasted_iota(jnp.int32, (L, ML_W), 1) // ML_DK
    ones_rows = jnp.ones((2 * COND_ROWS, L), BF16)
    zero_rows = jnp.zeros((2 * COND_ROWS, L), BF16)
    zero_half = jnp.zeros((ML_DK, L), BF16)
    seq_dirs = [(j, d) for j in range(nb) for d in range(2)]
    blocks = {jd: (fwd if jd[1] == 0 else bwd) for jd in seq_dirs}
    tri = [(col <= row).astype(BF16), (col >= row).astype(BF16)]
    valid_t = [row <= col, row >= col]

    bcum = {(j, d): _dot_exact_lhs(tri[d], jax.nn.log_sigmoid(blocks[(j, d)][5][j])) for j, d in seq_dirs}

    inter_t, w_t, a_parts = {}, {}, {}
    for j, d in seq_dirs:
        a = blocks[(j, d)][4][j] - bcum[(j, d)]
        m_vec = m_prev[(j, d)]
        b_last = bcum[(j, d)][L - 1:L, :] if d == 0 else bcum[(j, d)][0:1, :]
        mx = jnp.maximum(m_vec, jnp.max(a, axis=0, keepdims=True))
        m_next[(j, d)] = b_last + mx
        w_old = jnp.exp(m_vec - mx)
        inter_t[(j, d)] = _mxu_transpose(eye_lanes, _split2(bcum[(j, d)] + m_vec))
        w_t[(j, d)] = _mxu_transpose(eye_lanes, _split2(jnp.exp(a - mx)))
        a_ext = jnp.concatenate([a - m_vec, jnp.broadcast_to(w_old, (2 * COND_ROWS, LANES))], axis=0)
        a_parts[(j, d)] = _split2(a_ext)

    abc, st = {}, {}
    for j, d, h in units:
        pick = (sel_row == d * ML_HEADS + h).astype(BF16)
        parts = a_parts[(j, d)]
        abc[(j, d, h)] = _dot(parts[0], pick) + _dot(parts[1], pick)
        k_tok = blocks[(j, d)][0][j]
        st[(j, d, h)] = _dot(jnp.where(head_of_lane == h, k_tok, jnp.zeros_like(k_tok)), blocks[(j, d)][1][j])

    upd = {}
    for j, d, h in units:
        u = d * ML_HEADS + h
        kt_h = blocks[(j, d)][2][j, h * ML_DK:(h + 1) * ML_DK, :]
        v_h = blocks[(j, d)][3][j, h * ML_DK:(h + 1) * ML_DK, :]
        kw = (kt_h.astype(F32) * w_t[(j, d)][u:u + 1, :]).astype(BF16)
        upd[(j, d, h)] = _dot_nt(jnp.concatenate([v_h, ones_rows], axis=0),
                                 jnp.concatenate([kw, zero_half], axis=0))

    hts = {}
    for j, d, h in units:
        u = d * ML_HEADS + h
        ct, n_row = prev[(j, d, h)]
        w_old_u = abc[(j, d, h)][L:L + 1, :]
        dm = jnp.where(valid_t[d], abc[(j, d, h)][0:L], -jnp.inf)
        mp = jnp.maximum(jnp.max(dm, axis=0, keepdims=True), 0.0)
        w_inter = jnp.exp(-mp)
        qk_t = st[(j, d, h)] * jnp.exp(dm - mp)
        q_h = blocks[(j, d)][1][j, h * ML_DK:(h + 1) * ML_DK, :]
        v_h = blocks[(j, d)][3][j, h * ML_DK:(h + 1) * ML_DK, :]
        rhs = jnp.concatenate([qk_t.astype(BF16), (q_h.astype(F32) * w_inter).astype(BF16), zero_half], axis=0)
        lhs = jnp.concatenate([
            jnp.concatenate([v_h, ct.astype(BF16)], axis=1),
            jnp.concatenate([zero_rows, jnp.broadcast_to(n_row, (2 * COND_ROWS, LANES)).astype(BF16)], axis=1),
        ], axis=0)
        res = _dot(lhs, rhs)
        den = res[ML_DK:ML_DK + 1] + jnp.sum(qk_t, axis=0, keepdims=True)
        hts[(j, d, h)] = res[0:ML_DK] / jnp.maximum(jnp.abs(den), jnp.exp(-(inter_t[(j, d)][u:u + 1, :] + mp)))
        new[(j, d, h)] = (w_old_u * ct + upd[(j, d, h)][0:ML_DK],
                          w_old_u * n_row + upd[(j, d, h)][ML_DK:ML_DK + 1])

    for j, d in seq_dirs:
        ht_all = jnp.concatenate([hts[(j, d, h)] for h in range(ML_HEADS)], axis=0)
        h_outs[d][j] = _mxu_transpose(eye_l, _split2(ht_all))


    for (j, d, h), (ct_new, n_new) in new.items():
        ct_s[j, d, h] = ct_new
        n_s[j, d, h] = n_new
    for (j, d), m_new in m_next.items():
        m_s[j, d] = m_new

    if want_state:
        @pl.when(step == nc - 1)
        def _():
            pad_eye = _eye(ML_DK, LANES)
            for j, d, h in units:
                c_out[j, d, h] = _mxu_transpose(pad_eye, _split3(new[(j, d, h)][0]))
                n_out[j, d, h:h + 1, :] = new[(j, d, h)][1][:, 0:ML_DK]
            for (j, d), m_new in m_next.items():
                m_out[j, d] = m_new


def _mlstm(mk, mqt, mkt, mvt, gi, gf, batch, chunk, nb, init, l, want_state):
    t = mk.shape[0] // batch
    nc = t // chunk
    seq = lambda a: a.reshape(batch, t, a.shape[-1])
    in_specs, args = [], []
    for cidx in (lambda c: c, lambda c: nc - 1 - c):
        tok = lambda b, c, cidx=cidx: (b, cidx(c), 0)
        tr = lambda b, c, cidx=cidx: (b, 0, cidx(c))
        in_specs += [pl.BlockSpec((nb, chunk, ML_W), tok)] + [pl.BlockSpec((nb, ML_W, chunk), tr)] * 3
        in_specs += [pl.BlockSpec((nb, chunk, LANES), tok)] * 2
        args += [seq(mk), mqt, mkt, mvt, seq(gi), seq(gf)]
    has_init = init is not None
    if has_init:
        in_specs += [
            pl.BlockSpec((nb, None, 2, ML_HEADS, ML_DK, ML_DK), lambda b, c: (b, l, 0, 0, 0, 0)),
            pl.BlockSpec((nb, None, 2, ML_HEADS, ML_DK), lambda b, c: (b, l, 0, 0, 0)),
            pl.BlockSpec((nb, None, 2, 1, LANES), lambda b, c: (b, l, 0, 0, 0)),
        ]
        args += list(init)
    out_specs = [pl.BlockSpec((nb, chunk, ML_W), lambda b, c: (b, c, 0)),
                 pl.BlockSpec((nb, chunk, ML_W), lambda b, c: (b, nc - 1 - c, 0))]
    out_shape = [jax.ShapeDtypeStruct((batch, t, ML_W), F32)] * 2
    if want_state:
        out_specs += [
            pl.BlockSpec((nb, 2, ML_HEADS, ML_DK, ML_DK), lambda b, c: (b, 0, 0, 0, 0)),
            pl.BlockSpec((nb, 2, ML_HEADS, ML_DK), lambda b, c: (b, 0, 0, 0)),
            pl.BlockSpec((nb, 2, 1, LANES), lambda b, c: (b, 0, 0, 0)),
        ]
        out_shape += [jax.ShapeDtypeStruct((batch, 2, ML_HEADS, ML_DK, ML_DK), F32),
                      jax.ShapeDtypeStruct((batch, 2, ML_HEADS, ML_DK), F32),
                      jax.ShapeDtypeStruct((batch, 2, 1, LANES), F32)]
    outs = pl.pallas_call(
        functools.partial(_mlstm_kernel, chunk=chunk, nc=nc, nb=nb, has_init=has_init, want_state=want_state),
        grid=(batch // nb, nc),
        in_specs=in_specs,
        out_specs=out_specs,
        out_shape=out_shape,
        scratch_shapes=[pltpu.VMEM((nb, 2, ML_HEADS, ML_DK, LANES), F32), pltpu.VMEM((nb, 2, ML_HEADS, 1, LANES), F32),
                        pltpu.VMEM((nb, 2, 1, LANES), F32)],
        compiler_params=_params(("parallel", "arbitrary")),
        name="mlstm",
    )(*args)
    return [outs[0].reshape(batch * t, ML_W), outs[1].reshape(batch * t, ML_W)] + list(outs[2:])


def _outproj_kernel(x_ref, mod_ref, go_ref, hf_ref, hb_ref, mo_ref, lo_ref, onw_ref, seg_ref, w_ref, o_ref):
    gqa_w = GQA_HEADS * HEAD_DIM
    hsum = hf_ref[...] + hb_ref[...]
    hi, mid, lo = _split3(hsum * hsum)
    seg = seg_ref[...]
    ms = _dot(hi, seg) + _dot(mid, seg) + _dot(lo, seg)
    hn = hsum * lax.rsqrt(ms + EPS) * onw_ref[...]
    ml = (hn * jax.nn.sigmoid(mo_ref[...])).astype(BF16)
    out = (_dot(go_ref[...], w_ref[0:gqa_w, :]) + _dot(lo_ref[...], w_ref[gqa_w + ML_W:, :])
           + _dot(ml, w_ref[gqa_w:gqa_w + ML_W, :]))
    o_ref[...] = x_ref[...] + mod_ref[5:6, :] * out


def _outproj(x, mods, gqa_o, hf, hb, mo, mla_o, onw, seg, w_out, l, group_of_tile, tm):
    n, d = x.shape
    row = lambda i: (i, 0)
    lay = lambda shape: pl.BlockSpec((None,) + shape, lambda i: (l,) + (0,) * len(shape))
    return pl.pallas_call(
        _outproj_kernel,
        grid=(n // tm,),
        in_specs=[
            pl.BlockSpec((tm, d), row),
            pl.BlockSpec((None, None, N_MOD, d), lambda i: (l, group_of_tile(i), 0, 0)),
            pl.BlockSpec((tm, GQA_HEADS * HEAD_DIM), row),
            pl.BlockSpec((tm, ML_W), row), pl.BlockSpec((tm, ML_W), row), pl.BlockSpec((tm, ML_W), row),
            pl.BlockSpec((tm, MLA_HEADS * MLA_V), row),
            lay((1, ML_W)),
            pl.BlockSpec((ML_W, ML_W), lambda i: (0, 0)),
            lay((w_out.shape[1], d)),
        ],
        out_specs=pl.BlockSpec((tm, d), row),
        out_shape=jax.ShapeDtypeStruct((n, d), F32),
        compiler_params=_params(("parallel",)),
        name="outproj",
    )(x, mods, gqa_o, hf, hb, mo, mla_o, onw, seg, w_out)


def _final_norm_kernel(x_ref, w_ref, o_ref):
    x = x_ref[...]
    ms = jnp.mean(x * x, axis=-1, keepdims=True)
    o_ref[...] = x * lax.rsqrt(ms + EPS) * w_ref[...]


def _final_norm(x, w, tm):
    n, d = x.shape
    return pl.pallas_call(
        _final_norm_kernel,
        grid=(n // tm,),
        in_specs=[pl.BlockSpec((tm, d), lambda i: (i, 0)), pl.BlockSpec((1, d), lambda i: (0, 0))],
        out_specs=pl.BlockSpec((tm, d), lambda i: (i, 0)),
        out_shape=jax.ShapeDtypeStruct((n, d), F32),
        compiler_params=_params(("parallel",)),
        name="final_norm",
    )(x, w.reshape(1, d))


def _pad_last(a, width):
    return jnp.pad(a, [(0, 0)] * (a.ndim - 1) + [(0, width - a.shape[-1])])


def _pad_heads(a, heads, width):
    lead = a.shape[:-1]
    return _pad_last(a.reshape(lead + (heads, width)), LANES).reshape(lead + (heads * LANES,))


def _axial_rope(seq, rot_dim):
    half = rot_dim // 2
    rows = seq // GRID_W
    freqs = ROPE_BASE ** (-jnp.arange(0, half, 2, dtype=F32) / half)
    r = jnp.repeat(jnp.arange(rows, dtype=F32), GRID_W)
    c = jnp.tile(jnp.arange(GRID_W, dtype=F32), rows)
    ang = jnp.concatenate([r[:, None] * freqs, c[:, None] * freqs], axis=-1)
    return jnp.cos(ang), jnp.sin(ang)


def _rope_tables(seq):
    c, s = _axial_rope(seq, HEAD_DIM)
    ones = jnp.ones((seq, LANES - HEAD_DIM), F32)
    cq = jnp.concatenate([c, c, ones], axis=-1)
    sq = jnp.concatenate([-s, s, 0.0 * ones], axis=-1)
    c, s = _axial_rope(seq, MLA_ROPE)
    one_a = jnp.ones((seq, MLA_NOPE), F32)
    one_b = jnp.ones((seq, LANES - MLA_QK), F32)
    cm = jnp.concatenate([one_a, c, c, one_b], axis=-1)
    sm = jnp.concatenate([0.0 * one_a, -s, s, 0.0 * one_b], axis=-1)
    return cq, sq, cm, sm


def _pick_tile(n, pref):
    t = min(n, pref)
    while n % t:
        t //= 2
    return t


def kernel(x_prompt, x_sample, c, cache_gqa_k, cache_gqa_v, cache_mla_ckv, cache_mla_krope, state_mlstm_C, state_mlstm_n, state_mlstm_m, c_ctx, w_ada, b_ada, norm_w, ffn_w_gate, ffn_w_up, ffn_w_down, w_in, gqa_q_norm, gqa_k_norm, mlstm_gate_b, mlstm_out_norm, mla_q_norm, mla_w_uq, mla_kv_norm, mla_w_ukv, w_out, final_norm):
    bc, tc, d = x_prompt.shape
    bl, tl, _ = x_sample.shape
    depth = w_ada.shape[0]
    past = cache_gqa_k.shape[2]
    assert 1 + bl <= COND_ROWS

    cond = jnp.concatenate([c_ctx[None, :], c, jnp.zeros((COND_ROWS - 1 - bl, d), F32)], axis=0)
    mods = _ada(cond, w_ada, b_ada).reshape(depth, COND_ROWS, N_MOD, d)

    wg, wu, wd = ffn_w_gate.astype(BF16), ffn_w_up.astype(BF16), ffn_w_down.astype(BF16)
    idx = [0]
    for wdt in (GQA_HEADS * HEAD_DIM, GQA_KV_HEADS * HEAD_DIM, GQA_KV_HEADS * HEAD_DIM, ML_W, ML_W, ML_W, ML_W,
                4 * ML_HEADS, MLA_RANK, MLA_RANK, MLA_ROPE):
        idx.append(idx[-1] + wdt)
    seg = [w_in[..., a:b] for a, b in zip(idx[:-1], idx[1:])]
    s_gq, s_gk, s_gv, s_mq, s_mk, s_mv, s_mo, s_mg, s_ql, s_kv, s_kr = seg
    kr_slot_w = jnp.pad(s_kr, [(0, 0), (0, 0), (MLA_NOPE, LANES - MLA_QK)])
    nh = ML_HEADS
    gi_slot_w = _pad_last(jnp.concatenate([s_mg[..., 0:nh], s_mg[..., 2 * nh:3 * nh]], axis=-1), LANES)
    gf_slot_w = _pad_last(jnp.concatenate([s_mg[..., nh:2 * nh], s_mg[..., 3 * nh:4 * nh]], axis=-1), LANES)
    w_in_p = jnp.concatenate(
        [_pad_heads(s_gq, GQA_HEADS, HEAD_DIM), _pad_heads(s_gk, GQA_KV_HEADS, HEAD_DIM), s_gv,
         s_mk, s_mo, s_ql, s_kv, kr_slot_w, gi_slot_w, gf_slot_w], axis=-1).astype(BF16)
    assert w_in_p.shape[-1] == D_INP
    wmt = jnp.swapaxes(jnp.concatenate([s_mq, s_mk, s_mv], axis=-1), 1, 2).astype(BF16)
    wq = _pad_heads(mla_w_uq, MLA_HEADS, MLA_QK).astype(BF16)
    ukv = mla_w_ukv.reshape(depth, MLA_RANK, MLA_HEADS, MLA_NOPE + MLA_V)
    wk = _pad_last(ukv[..., :MLA_NOPE], LANES).reshape(depth, MLA_RANK, MLA_HEADS * LANES).astype(BF16)
    wvt = jnp.swapaxes(ukv[..., MLA_NOPE:].reshape(depth, MLA_RANK, MLA_HEADS * MLA_V), 1, 2).astype(BF16)
    wgvt = jnp.swapaxes(s_gv, 1, 2).astype(BF16)
    w_out_b = w_out.astype(BF16)
    qn = _pad_last(gqa_q_norm, LANES).reshape(depth, 1, LANES)
    kn = _pad_last(gqa_k_norm, LANES).reshape(depth, 1, LANES)
    gbi = _pad_last(jnp.concatenate([mlstm_gate_b[:, 0], mlstm_gate_b[:, 2]], axis=-1), LANES).reshape(depth, 1, LANES)
    gbf = _pad_last(jnp.concatenate([mlstm_gate_b[:, 1], mlstm_gate_b[:, 3]], axis=-1), LANES).reshape(depth, 1, LANES)
    mqn = mla_q_norm.reshape(depth, 1, MLA_RANK)
    mkvn = mla_kv_norm.reshape(depth, 1, MLA_RANK)
    onw = mlstm_out_norm.reshape(depth, 1, ML_W)
    norm_w4 = norm_w.reshape(depth, 3, 1, d)
    head_id = jnp.arange(ML_W) // ML_DK
    seg_mean = ((head_id[:, None] == head_id[None, :]).astype(F32) / ML_DK).astype(BF16)
    rope_tabs = _rope_tables(tl)

    ck_gqa = _pad_heads(cache_gqa_k.reshape(bl, depth, past, GQA_KV_HEADS * HEAD_DIM), GQA_KV_HEADS,
                        HEAD_DIM).astype(BF16)
    cvt_gqa = jnp.transpose(cache_gqa_v.reshape(bl * depth, past, GQA_KV_HEADS * HEAD_DIM), (2, 0, 1)).reshape(
        GQA_KV_HEADS * HEAD_DIM, bl, depth, past).astype(BF16)
    ckr_slot = jnp.pad(cache_mla_krope, [(0, 0), (0, 0), (0, 0), (MLA_NOPE, LANES - MLA_QK)])
    m0_lanes = jnp.stack([jnp.pad(state_mlstm_m[:, :, dd], [(0, 0), (0, 0), (dd * nh, LANES - (dd + 1) * nh)])
                          for dd in range(2)], axis=2)[:, :, :, None, :]

    nc_rows, nl_rows = bc * tc, bl * tl
    tm_c, tm_l = _pick_tile(nc_rows, 512), _pick_tile(tl, 512)
    tm_ci = _pick_tile(nc_rows, 512)
    assert tm_ci % tc == 0 or tc % tm_ci == 0
    tq_c, tq_l = _pick_tile(tc, 512), _pick_tile(tl, 512)
    nseq_c = _pick_tile(bc, 4) if tq_c == tc else 1
    chunk_c = chunk_l = LANES
    assert tc % LANES == 0 and tl % LANES == 0
    nb_c, nb_l = _pick_tile(bc, 4), _pick_tile(bl, 2)
    ctx_group = lambda i: 0
    lat_group = lambda i: 1 + i // (tl // tm_l)

    xc = x_prompt.reshape(nc_rows, d)
    xl = x_sample.reshape(nl_rows, d)
    collected = [[] for _ in range(7)]
    for l in range(depth):
        xc = _ffn(xc, mods, norm_w4, wg, wu, wd, l, 0, ctx_group, tm_c)
        xl = _ffn(xl, mods, norm_w4, wg, wu, wd, l, 0, lat_group, tm_l)

        (gq, gk, gv, mk, mqt, mkt, mvt, mo, gi, gf, qm, kp, vm, gkf, gvf, ckv, krs) = _inproj(
            xc, mods, norm_w4, w_in_p, wgvt, wmt, qn, kn, gbi, gbf, mqn, mkvn, wq, wk, wvt, None, l, ctx_group, tm_ci,
            tc, True)
        gqa_o = _attn(gq, [gk], [gv], bc, GQA_HEADS, GQA_GROUP, tq_c, nseq_c)
        mla_o = _attn(qm, [kp], [vm], bc, MLA_HEADS, 1, tq_c, nseq_c)
        hf, hb, c_new, n_new, m_lanes = _mlstm(mk, mqt, mkt, mvt, gi, gf, bc, chunk_c, nb_c, None, l, True)
        m_new = jnp.stack([m_lanes[:, dd, 0, dd * nh:(dd + 1) * nh] for dd in range(2)], axis=1)
        xc = _outproj(xc, mods, gqa_o, hf, hb, mo, mla_o, onw, seg_mean, w_out_b, l, ctx_group, tm_c)
        new_k = gkf.reshape(bc, tc, GQA_KV_HEADS, LANES)[..., :HEAD_DIM]
        new_v = gvf.reshape(bc, tc, GQA_KV_HEADS, HEAD_DIM)
        new_ckv = ckv.reshape(bc, tc, MLA_RANK)
        new_kr = krs.reshape(bc, tc, LANES)[..., MLA_NOPE:MLA_QK]
        for lst, t in zip(collected, (new_k, new_v, new_ckv, new_kr, c_new, n_new, m_new)):
            lst.append(t)

        (gq, gk, gv, mk, mqt, mkt, mvt, mo, gi, gf, qm, kp, vm) = _inproj(
            xl, mods, norm_w4, w_in_p, wgvt, wmt, qn, kn, gbi, gbf, mqn, mkvn, wq, wk, wvt, rope_tabs, l, lat_group,
            tm_l, tl, False)
        kp_c, vm_c = _kvexp(cache_mla_ckv[:, l].reshape(bl * past, MLA_RANK), ckr_slot[:, l].reshape(bl * past, LANES),
                            wk, wvt, l)
        gqa_o = _attn(gq, [ck_gqa[:, l].reshape(bl * past, -1), gk], [cvt_gqa[:, :, l].reshape(-1, bl * past), gv],
                      bl, GQA_HEADS, GQA_GROUP, tq_l, 1)
        mla_o = _attn(qm, [kp_c, kp], [vm_c, vm], bl, MLA_HEADS, 1, tq_l, 1)
        hf, hb = _mlstm(mk, mqt, mkt, mvt, gi, gf, bl, chunk_l, nb_l, (state_mlstm_C, state_mlstm_n, m0_lanes), l,
                        False)
        xl = _outproj(xl, mods, gqa_o, hf, hb, mo, mla_o, onw, seg_mean, w_out_b, l, lat_group, tm_l)

        xc = _ffn(xc, mods, norm_w4, wg, wu, wd, l, 1, ctx_group, tm_c)
        xl = _ffn(xl, mods, norm_w4, wg, wu, wd, l, 1, lat_group, tm_l)

    y_prompt = _final_norm(xc, final_norm, tm_c).reshape(bc, tc, d)
    y_sample = _final_norm(xl, final_norm, tm_l).reshape(bl, tl, d)
    stacked = [jnp.stack(lst, axis=1) for lst in collected]
    return (y_prompt, y_sample, *stacked)
```

```python
import functools
from typing import NamedTuple

import jax
import jax.numpy as jnp
from jax import lax
from jax.experimental import pallas as pl
from jax.experimental.pallas import tpu as pltpu

F32 = jnp.float32
BF16 = jnp.bfloat16

EPS = 1e-6
ROPE_BASE = 10000.0
GRID_W = 64
N_MOD = 9
HEAD_DIM = 64
GQA_HEADS = 6
GQA_KV_HEADS = 2
GQA_GROUP = GQA_HEADS // GQA_KV_HEADS
ML_HEADS = 4
ML_DK = 64
ML_W = ML_HEADS * ML_DK
MLA_HEADS = 6
MLA_RANK = 256
MLA_NOPE = 64
MLA_ROPE = 32
MLA_V = 64
MLA_QK = MLA_NOPE + MLA_ROPE
LANES = 128
COND_ROWS = 8
VMEM_LIMIT = 48 * 1024 * 1024
LOG2E = 1.4426950408889634
ROW_TILE = 512
SEQS_PER_STEP = 4
REDUCE_ROWS = 64

O_GQ = 0
O_GK = O_GQ + GQA_HEADS * LANES
O_GV = O_GK + GQA_KV_HEADS * LANES
O_MK = O_GV + GQA_KV_HEADS * HEAD_DIM
O_MO = O_MK + ML_W
O_QL = O_MO + ML_W
O_KV = O_QL + MLA_RANK
O_KR = O_KV + MLA_RANK
O_GI = O_KR + LANES
O_GF = O_GI + LANES
D_INP = O_GF + LANES


def _params(sem):
    return pltpu.CompilerParams(dimension_semantics=sem, vmem_limit_bytes=VMEM_LIMIT)


def _norm_mod(x, nw, sc, sh):
    ms = jnp.mean(x * x, axis=-1, keepdims=True)
    return (x * lax.rsqrt(ms + EPS) * nw) * (1.0 + sc) + sh


def _dot(a, b):
    return jnp.dot(a, b, preferred_element_type=F32)


def _dot_nt(a, b):
    return lax.dot_general(a, b, (((1,), (1,)), ((), ())), preferred_element_type=F32)


def _split2(x):
    hi = x.astype(BF16)
    return hi, (x - hi.astype(F32)).astype(BF16)


def _split3(x):
    hi = x.astype(BF16)
    r1 = x - hi.astype(F32)
    mid = r1.astype(BF16)
    lo = (r1 - mid.astype(F32)).astype(BF16)
    return hi, mid, lo


def _dot_exact_lhs(a_bf16, x):
    hi, mid, lo = _split3(x)
    return _dot(a_bf16, hi) + _dot(a_bf16, mid) + _dot(a_bf16, lo)


def _ada_kernel(cond_ref, w_ref, b_ref, o_ref):
    c = cond_ref[...]
    s = (c * jax.nn.sigmoid(c)).astype(BF16)
    o_ref[...] = _dot(s, w_ref[...].astype(BF16)) + b_ref[...]


def _ada(cond, w_ada, b_ada):
    depth, d, nd = w_ada.shape
    tn = d
    return pl.pallas_call(
        _ada_kernel,
        grid=(depth, nd // tn),
        in_specs=[
            pl.BlockSpec((COND_ROWS, d), lambda l, j: (0, 0)),
            pl.BlockSpec((None, d, tn), lambda l, j: (l, 0, j)),
            pl.BlockSpec((None, 1, tn), lambda l, j: (l, 0, j)),
        ],
        out_specs=pl.BlockSpec((None, COND_ROWS, tn), lambda l, j: (l, 0, j)),
        out_shape=jax.ShapeDtypeStruct((depth, COND_ROWS, nd), F32),
        compiler_params=_params(("parallel", "parallel")),
        name="ada",
    )(cond, w_ada, b_ada.reshape(depth, 1, nd))


def _ffn_kernel(x_ref, mod_ref, nw_ref, wg_ref, wu_ref, wd_ref, *rest, mi):
    o_ref = rest[-1]
    x = x_ref[...]
    h = _norm_mod(x, nw_ref[...], mod_ref[mi + 1:mi + 2, :], mod_ref[mi:mi + 1, :]).astype(BF16)
    g = _dot(h, wg_ref[...])
    u = _dot(h, wu_ref[...])
    a = (g * jax.nn.sigmoid(g) * u).astype(BF16)
    y = x + (0.5 * mod_ref[mi + 2:mi + 3, :]) * _dot(a, wd_ref[...])
    if len(rest) == 2:
        y = y * lax.rsqrt(jnp.mean(y * y, axis=-1, keepdims=True) + EPS) * rest[0][...]
    o_ref[...] = y


def _ffn(x, mods, norm_w4, wg, wu, wd, l, j, group_of_tile, tm, final_w=None):
    n, d = x.shape
    dff = wg.shape[-1]
    mi = 0 if j == 0 else 6
    nwi = 0 if j == 0 else 2
    resident = lambda shape: pl.BlockSpec((None, None) + shape, lambda i: (l, j, 0, 0),
                                          pipeline_mode=pl.Buffered(1))
    return pl.pallas_call(
        functools.partial(_ffn_kernel, mi=mi),
        grid=(n // tm,),
        in_specs=[
            pl.BlockSpec((tm, d), lambda i: (i, 0)),
            pl.BlockSpec((None, None, N_MOD, d), lambda i: (l, group_of_tile(i), 0, 0)),
            pl.BlockSpec((None, None, 1, d), lambda i: (l, nwi, 0, 0)),
            resident((d, dff)), resident((d, dff)), resident((dff, d)),
        ] + ([] if final_w is None else [pl.BlockSpec((1, d), lambda i: (0, 0))]),
        out_specs=pl.BlockSpec((tm, d), lambda i: (i, 0)),
        out_shape=jax.ShapeDtypeStruct((n, d), F32),
        compiler_params=_params(("parallel",)),
        name="ffn",
    )(x, mods, norm_w4, wg, wu, wd, *([] if final_w is None else [final_w.reshape(1, d)]))


def _inproj_kernel(*refs, rope, ctx_out):
    it = iter(refs)
    (x_ref, mod_ref, nw_ref, w_ref, wgvt_ref, wmt_ref, qn_ref, kn_ref, gbi_ref, gbf_ref, mqn_ref, mkvn_ref, wq_ref,
     wk_ref, wvt_ref) = (next(it) for _ in range(15))
    if rope:
        cq_ref, sq_ref, cm_ref, sm_ref = (next(it) for _ in range(4))
    gq_o, gk_o, gv_o, mk_o, mqt_o, mkt_o, mvt_o, mo_o, gi_o, gf_o, qm_o, kp_o, vm_o = (next(it) for _ in range(13))
    if ctx_out:
        gkf_o, gvf_o, ckv_o, kr_o = (next(it) for _ in range(4))

    x = x_ref[...]
    tm = x.shape[0]
    hb = _norm_mod(x, nw_ref[...], mod_ref[4:5, :], mod_ref[3:4, :]).astype(BF16)

    def proj(o, n):
        return _dot(hb, w_ref[:, o:o + n])

    lane = lax.broadcasted_iota(jnp.int32, (tm, LANES), 1)

    def head_norm(xg, wrow):
        ms = jnp.sum(xg * xg, axis=-1, keepdims=True) * (1.0 / HEAD_DIM)
        return xg * lax.rsqrt(ms + EPS) * wrow

    def rope_hd(xg):
        half = HEAD_DIM // 2
        partner = jnp.where(lane < half, pltpu.roll(xg, LANES - half, 1), pltpu.roll(xg, half, 1))
        return xg * cq_ref[...] + partner * sq_ref[...]

    def rope_mla(xg):
        half = MLA_ROPE // 2
        partner = jnp.where((lane & half) == 0, pltpu.roll(xg, LANES - half, 1), pltpu.roll(xg, half, 1))
        return xg * cm_ref[...] + partner * sm_ref[...]

    gq = proj(O_GQ, GQA_HEADS * LANES)
    for g in range(GQA_HEADS):
        qg = head_norm(gq[:, g * LANES:(g + 1) * LANES], qn_ref[...])
        if rope:
            qg = rope_hd(qg)
        gq_o[:, g * LANES:(g + 1) * LANES] = (qg * (LOG2E * HEAD_DIM ** -0.5)).astype(BF16)
    gk = proj(O_GK, GQA_KV_HEADS * LANES)
    kg_ctx = []
    for g in range(GQA_KV_HEADS):
        kg = head_norm(gk[:, g * LANES:(g + 1) * LANES], kn_ref[...])
        if ctx_out:
            kg_ctx.append(kg)
        if rope:
            kg = rope_hd(kg)
        gk_o[:, g * LANES:(g + 1) * LANES] = kg.astype(BF16)
    gv_o[...] = _dot_nt(wgvt_ref[...], hb).astype(BF16)
    if ctx_out:
        gkf_o[...] = jnp.where(lane < HEAD_DIM, kg_ctx[0], pltpu.roll(kg_ctx[1], HEAD_DIM, 1))
        gvf_o[...] = proj(O_GV, GQA_KV_HEADS * HEAD_DIM)

    mk_o[...] = proj(O_MK, ML_W).astype(BF16)
    mt = _dot_nt(wmt_ref[...], hb)
    rs = mqt_o.shape[2]
    for s in range(mqt_o.shape[0]):
        cols = slice(s * rs, (s + 1) * rs)
        mqt_o[s] = (mt[0:ML_W, cols] * (ML_DK ** -0.5)).astype(BF16)
        mkt_o[s] = mt[ML_W:2 * ML_W, cols].astype(BF16)
        mvt_o[s] = mt[2 * ML_W:3 * ML_W, cols].astype(BF16)
    mo_o[...] = proj(O_MO, ML_W)
    gi_o[...] = proj(O_GI, LANES) + gbi_ref[...]
    gf_o[...] = proj(O_GF, LANES) + gbf_ref[...]

    xq = proj(O_QL, MLA_RANK)
    ms = jnp.mean(xq * xq, axis=-1, keepdims=True)
    qn = (xq * lax.rsqrt(ms + EPS) * mqn_ref[...]).astype(BF16)
    qm = _dot(qn, wq_ref[...])
    for g in range(MLA_HEADS):
        qg = qm[:, g * LANES:(g + 1) * LANES]
        if rope:
            qg = rope_mla(qg)
        qm_o[:, g * LANES:(g + 1) * LANES] = (qg * (LOG2E * MLA_QK ** -0.5)).astype(BF16)

    xkv = proj(O_KV, MLA_RANK)
    ms = jnp.mean(xkv * xkv, axis=-1, keepdims=True)
    ckv = xkv * lax.rsqrt(ms + EPS) * mkvn_ref[...]
    kr = proj(O_KR, LANES)
    if rope:
        kr = rope_mla(kr)
    if ctx_out:
        ckv_o[...] = ckv
        kr_o[...] = kr
    cb = ckv.astype(BF16)
    kk = _dot(cb, wk_ref[...])
    for g in range(MLA_HEADS):
        kp_o[:, g * LANES:(g + 1) * LANES] = (kk[:, g * LANES:(g + 1) * LANES] + kr).astype(BF16)
    vm_o[...] = _dot_nt(wvt_ref[...], cb).astype(BF16)


def _inproj(x, mods, norm_w4, w_in_p, wgvt, wmt, qn, kn, gbi, gbf, mqn, mkvn, wq, wk, wvt, rope_tabs, l, group_of_tile,
            tm, rows_per_seq, ctx_out):
    n, d = x.shape
    rope = rope_tabs is not None
    tiles_per_seq = max(1, rows_per_seq // tm)
    seqs_per_tile = max(1, tm // rows_per_seq)
    rows_in_tile = tm // seqs_per_tile
    row = lambda i: (i, 0)
    lay = lambda shape: pl.BlockSpec((None,) + shape, lambda i: (l,) + (0,) * len(shape))
    in_specs = [
        pl.BlockSpec((tm, d), row),
        pl.BlockSpec((None, None, N_MOD, d), lambda i: (l, group_of_tile(i), 0, 0)),
        pl.BlockSpec((None, None, 1, d), lambda i: (l, 1, 0, 0)),
        lay((d, D_INP)), lay((GQA_KV_HEADS * HEAD_DIM, d)), lay((3 * ML_W, d)),
        lay((1, LANES)), lay((1, LANES)), lay((1, LANES)), lay((1, LANES)), lay((1, MLA_RANK)), lay((1, MLA_RANK)),
        lay((MLA_RANK, MLA_HEADS * LANES)), lay((MLA_RANK, MLA_HEADS * LANES)), lay((MLA_HEADS * MLA_V, MLA_RANK)),
    ]
    args = [x, mods, norm_w4, w_in_p, wgvt, wmt, qn, kn, gbi, gbf, mqn, mkvn, wq, wk, wvt]
    if rope:
        tab = pl.BlockSpec((tm, LANES), lambda i: (i % tiles_per_seq, 0))
        in_specs += [tab] * 4
        args += list(rope_tabs)
    widths = [(GQA_HEADS * LANES, BF16, "row"), (GQA_KV_HEADS * LANES, BF16, "row"),
              (GQA_KV_HEADS * HEAD_DIM, BF16, "col"),
              (ML_W, BF16, "row"), (ML_W, BF16, "seq"), (ML_W, BF16, "seq"), (ML_W, BF16, "seq"),
              (ML_W, F32, "row"), (LANES, F32, "row"), (LANES, F32, "row"),
              (MLA_HEADS * LANES, BF16, "row"), (MLA_HEADS * LANES, BF16, "row"), (MLA_HEADS * MLA_V, BF16, "col")]
    if ctx_out:
        widths += [(GQA_KV_HEADS * HEAD_DIM, F32, "row"), (GQA_KV_HEADS * HEAD_DIM, F32, "row"),
                   (MLA_RANK, F32, "row"), (LANES, F32, "row")]
    spec = {"row": lambda w: pl.BlockSpec((tm, w), row),
            "col": lambda w: pl.BlockSpec((w, tm), lambda i: (0, i)),
            "seq": lambda w: pl.BlockSpec((seqs_per_tile, w, rows_in_tile),
                                          lambda i: (i // tiles_per_seq, 0, i % tiles_per_seq))}
    shape = {"row": lambda w: (n, w), "col": lambda w: (w, n), "seq": lambda w: (n // rows_per_seq, w, rows_per_seq)}
    return pl.pallas_call(
        functools.partial(_inproj_kernel, rope=rope, ctx_out=ctx_out),
        grid=(n // tm,),
        in_specs=in_specs,
        out_specs=[spec[kind](w) for w, _, kind in widths],
        out_shape=[jax.ShapeDtypeStruct(shape[kind](w), dt) for w, dt, kind in widths],
        compiler_params=_params(("parallel",)),
        name="inproj",
    )(*args)


def _kvexp_kernel(ckv_ref, kr_ref, wk_ref, wvt_ref, kp_o, vm_o):
    cb = ckv_ref[...].astype(BF16)
    kk = _dot(cb, wk_ref[...])
    kr = kr_ref[...]
    for g in range(MLA_HEADS):
        kp_o[:, g * LANES:(g + 1) * LANES] = (kk[:, g * LANES:(g + 1) * LANES] + kr).astype(BF16)
    vm_o[...] = _dot_nt(wvt_ref[...], cb).astype(BF16)


def _kvexp(ckv, kr_slot, wk, wvt, l):
    n = ckv.shape[0]
    lay = lambda shape: pl.BlockSpec((None,) + shape, lambda i: (l,) + (0,) * len(shape))
    return pl.pallas_call(
        _kvexp_kernel,
        grid=(1,),
        in_specs=[pl.BlockSpec((n, MLA_RANK), lambda i: (0, 0)), pl.BlockSpec((n, LANES), lambda i: (0, 0)),
                  lay((MLA_RANK, MLA_HEADS * LANES)), lay((MLA_HEADS * MLA_V, MLA_RANK))],
        out_specs=[pl.BlockSpec((n, MLA_HEADS * LANES), lambda i: (0, 0)),
                   pl.BlockSpec((MLA_HEADS * MLA_V, n), lambda i: (0, 0))],
        out_shape=[jax.ShapeDtypeStruct((n, MLA_HEADS * LANES), BF16),
                   jax.ShapeDtypeStruct((MLA_HEADS * MLA_V, n), BF16)],
        compiler_params=_params(("arbitrary",)),
        name="kvexp",
    )(ckv, kr_slot, wk, wvt)


def _attn_kernel(*refs, n_src, heads, group, nseq, tq):
    q_ref = refs[0]
    k_refs = refs[1:1 + n_src]
    vt_refs = refs[1 + n_src:1 + 2 * n_src]
    o_ref = refs[1 + 2 * n_src]
    seq_rows = [k.shape[0] // nseq for k in k_refs]

    def reduce_rows(x, op):
        r = x.shape[0]
        part = op(x.reshape(r // REDUCE_ROWS, REDUCE_ROWS, x.shape[1]), axis=0) if r > REDUCE_ROWS else x
        return op(part, axis=0, keepdims=True)

    def scores(j, h):
        kvh = h // group
        qh = q_ref[j * tq:(j + 1) * tq, h * LANES:(h + 1) * LANES]
        return [_dot_nt(k[j * s:(j + 1) * s, kvh * LANES:(kvh + 1) * LANES], qh)
                for k, s in zip(k_refs, seq_rows)]

    units = [(j, h) for j in range(nseq) for h in range(heads)]
    sts_next = scores(*units[0])
    head_outs = []
    for idx, (j, h) in enumerate(units):
        kvh = h // group
        sts = sts_next
        if idx + 1 < len(units):
            sts_next = scores(*units[idx + 1])
        m = reduce_rows(sts[0], jnp.max)
        for st in sts[1:]:
            m = jnp.maximum(m, reduce_rows(st, jnp.max))
        den = None
        acc = None
        for st, vt, s in zip(sts, vt_refs, seq_rows):
            p = jnp.exp2(st - m)
            ps = reduce_rows(p, jnp.sum)
            pv = _dot(vt[kvh * MLA_V:(kvh + 1) * MLA_V, j * s:(j + 1) * s], p.astype(BF16))
            den = ps if den is None else den + ps
            acc = pv if acc is None else acc + pv
        head_outs.append(acc / den)
        if h == heads - 1:
            pairs = [jnp.concatenate(head_outs[i:i + 2], axis=0).T for i in range(0, heads, 2)]
            o_ref[j * tq:(j + 1) * tq, :] = jnp.concatenate(pairs, axis=-1).astype(BF16)
            head_outs = []


def _attn(q, ks, vts, batch, heads, group, tq, nseq):
    n = q.shape[0]
    t = n // batch
    nq = t // tq
    assert nseq == 1 or nq == 1
    n_src = len(ks)
    in_specs = [pl.BlockSpec((nseq * tq, heads * LANES), lambda b, i: (b * nq + i, 0))]
    for a in ks:
        in_specs.append(pl.BlockSpec((nseq * a.shape[0] // batch, a.shape[1]), lambda b, i: (b, 0)))
    for a in vts:
        in_specs.append(pl.BlockSpec((a.shape[0], nseq * a.shape[1] // batch), lambda b, i: (0, b)))
    return pl.pallas_call(
        functools.partial(_attn_kernel, n_src=n_src, heads=heads, group=group, nseq=nseq, tq=tq),
        grid=(batch // nseq, nq),
        in_specs=in_specs,
        out_specs=pl.BlockSpec((nseq * tq, heads * MLA_V), lambda b, i: (b * nq + i, 0)),
        out_shape=jax.ShapeDtypeStruct((n, heads * MLA_V), BF16),
        compiler_params=_params(("parallel", "parallel")),
        name="attn",
    )(q, *ks, *vts)


def _mxu_transpose(eye, parts):
    out = _dot_nt(eye, parts[0])
    for p in parts[1:]:
        out = out + _dot_nt(eye, p)
    return out


def _eye(n, m):
    return (lax.broadcasted_iota(jnp.int32, (n, m), 0) == lax.broadcasted_iota(jnp.int32, (n, m), 1)).astype(BF16)


def _mlstm_kernel(*refs, chunk, nc, nb, has_init, want_state):
    it = iter(refs)
    fwd = tuple(next(it) for _ in range(6))
    bwd = tuple(next(it) for _ in range(6))
    if has_init:
        c0_ref, n0_ref, m0_ref = (next(it) for _ in range(3))
    h_outs = (next(it), next(it))
    if want_state:
        c_out, n_out, m_out = (next(it) for _ in range(3))
    ct_s, n_s, m_s = (next(it) for _ in range(3))
    step = pl.program_id(1)
    L = chunk
    eye_l = _eye(L, L)
    eye_lanes = _eye(LANES, LANES)
    eye_dk = _eye(ML_DK, ML_DK)
    units = [(j, d, h) for j in range(nb) for d in range(2) for h in range(ML_HEADS)]

    @pl.when(step == 0)
    def _():
        if has_init:
            zpad = jnp.zeros((ML_DK, ML_DK), F32)
            for j, d, h in units:
                c0 = jnp.concatenate([c0_ref[j, d, h], zpad], axis=0)
                ct_s[j, d, h] = _mxu_transpose(eye_dk, _split3(c0))
                n_s[j, d, h] = jnp.concatenate([n0_ref[j, d, h:h + 1, :], zpad[0:1]], axis=1)
            m_s[...] = m0_ref[...]
        else:
            ct_s[...] = jnp.zeros_like(ct_s)
            n_s[...] = jnp.zeros_like(n_s)
            m_s[...] = jnp.zeros_like(m_s)

    prev = {u: (ct_s[u[0], u[1], u[2]], n_s[u[0], u[1], u[2]]) for u in units}
    m_prev = {(j, d): m_s[j, d] for j in range(nb) for d in range(2)}
    new, m_next = {}, {}

    row = lax.broadcasted_iota(jnp.int32, (L, L), 0)
    col = lax.broadcasted_iota(jnp.int32, (L, L), 1)
    sel_row = lax.broadcasted_iota(jnp.int32, (LANES, L), 0)
    head_of_lane = lax.broadcasted_iota(jnp.int32, (L, ML_W), 1) // ML_DK
    ones_rows = jnp.ones((2 * COND_ROWS, L), BF16)
    zero_rows = jnp.zeros((2 * COND_ROWS, L), BF16)
    zero_half = jnp.zeros((ML_DK, L), BF16)
    seq_dirs = [(j, d) for j in range(nb) for d in range(2)]
    blocks = {jd: (fwd if jd[1] == 0 else bwd) for jd in seq_dirs}
    tri = [(col <= row).astype(BF16), (col >= row).astype(BF16)]
    valid_t = [row <= col, row >= col]

    bcum = {(j, d): _dot_exact_lhs(tri[d], jax.nn.log_sigmoid(blocks[(j, d)][5][j])) for j, d in seq_dirs}

    inter_t, w_t, a_parts = {}, {}, {}
    for j, d in seq_dirs:
        a = blocks[(j, d)][4][j] - bcum[(j, d)]
        m_vec = m_prev[(j, d)]
        b_last = bcum[(j, d)][L - 1:L, :] if d == 0 else bcum[(j, d)][0:1, :]
        mx = jnp.maximum(m_vec, jnp.max(a, axis=0, keepdims=True))
        m_next[(j, d)] = b_last + mx
        w_old = jnp.exp(m_vec - mx)
        inter_t[(j, d)] = _mxu_transpose(eye_lanes, _split2(bcum[(j, d)] + m_vec))
        w_t[(j, d)] = _mxu_transpose(eye_lanes, _split2(jnp.exp(a - mx)))
        a_ext = jnp.concatenate([a - m_vec, jnp.broadcast_to(w_old, (2 * COND_ROWS, LANES))], axis=0)
        a_parts[(j, d)] = _split2(a_ext)

    abc, st = {}, {}
    for j, d, h in units:
        pick = (sel_row == d * ML_HEADS + h).astype(BF16)
        parts = a_parts[(j, d)]
        abc[(j, d, h)] = _dot(parts[0], pick) + _dot(parts[1], pick)
        k_tok = blocks[(j, d)][0][j]
        st[(j, d, h)] = _dot(jnp.where(head_of_lane == h, k_tok, jnp.zeros_like(k_tok)), blocks[(j, d)][1][j])

    upd = {}
    for j, d, h in units:
        u = d * ML_HEADS + h
        kt_h = blocks[(j, d)][2][j, h * ML_DK:(h + 1) * ML_DK, :]
        v_h = blocks[(j, d)][3][j, h * ML_DK:(h + 1) * ML_DK, :]
        kw = (kt_h.astype(F32) * w_t[(j, d)][u:u + 1, :]).astype(BF16)
        upd[(j, d, h)] = _dot_nt(jnp.concatenate([v_h, ones_rows], axis=0),
                                 jnp.concatenate([kw, zero_half], axis=0))

    hts = {}
    for j, d, h in units:
        u = d * ML_HEADS + h
        ct, n_row = prev[(j, d, h)]
        w_old_u = abc[(j, d, h)][L:L + 1, :]
        dm = jnp.where(valid_t[d], abc[(j, d, h)][0:L], -jnp.inf)
        mp = jnp.maximum(jnp.max(dm, axis=0, keepdims=True), 0.0)
        w_inter = jnp.exp(-mp)
        qk_t = st[(j, d, h)] * jnp.exp(dm - mp)
        q_h = blocks[(j, d)][1][j, h * ML_DK:(h + 1) * ML_DK, :]
        v_h = blocks[(j, d)][3][j, h * ML_DK:(h + 1) * ML_DK, :]
        rhs = jnp.concatenate([qk_t.astype(BF16), (q_h.astype(F32) * w_inter).astype(BF16), zero_half], axis=0)
        lhs = jnp.concatenate([
            jnp.concatenate([v_h, ct.astype(BF16)], axis=1),
            jnp.concatenate([zero_rows, jnp.broadcast_to(n_row, (2 * COND_ROWS, LANES)).astype(BF16)], axis=1),
        ], axis=0)
        res = _dot(lhs, rhs)
        den = res[ML_DK:ML_DK + 1] + jnp.sum(qk_t, axis=0, keepdims=True)
        hts[(j, d, h)] = res[0:ML_DK] / jnp.maximum(jnp.abs(den), jnp.exp(-(inter_t[(j, d)][u:u + 1, :] + mp)))
        new[(j, d, h)] = (w_old_u * ct + upd[(j, d, h)][0:ML_DK],
                          w_old_u * n_row + upd[(j, d, h)][ML_DK:ML_DK + 1])

    for j, d in seq_dirs:
        ht_all = jnp.concatenate([hts[(j, d, h)] for h in range(ML_HEADS)], axis=0)
        h_outs[d][j] = _mxu_transpose(eye_l, _split2(ht_all))


    for (j, d, h), (ct_new, n_new) in new.items():
        ct_s[j, d, h] = ct_new
        n_s[j, d, h] = n_new
    for (j, d), m_new in m_next.items():
        m_s[j, d] = m_new

    if want_state:
        @pl.when(step == nc - 1)
        def _():
            pad_eye = _eye(ML_DK, LANES)
            for j, d, h in units:
                c_out[j, d, h] = _mxu_transpose(pad_eye, _split3(new[(j, d, h)][0]))
                n_out[j, d, h:h + 1, :] = new[(j, d, h)][1][:, 0:ML_DK]
            for (j, d), m_new in m_next.items():
                m_out[j, d] = m_new


def _mlstm(mk, mqt, mkt, mvt, gi, gf, batch, chunk, nb, init, l, want_state):
    t = mk.shape[0] // batch
    nc = t // chunk
    seq = lambda a: a.reshape(batch, t, a.shape[-1])
    in_specs, args = [], []
    for cidx in (lambda c: c, lambda c: nc - 1 - c):
        tok = lambda b, c, cidx=cidx: (b, cidx(c), 0)
        tr = lambda b, c, cidx=cidx: (b, 0, cidx(c))
        in_specs += [pl.BlockSpec((nb, chunk, ML_W), tok)] + [pl.BlockSpec((nb, ML_W, chunk), tr)] * 3
        in_specs += [pl.BlockSpec((nb, chunk, LANES), tok)] * 2
        args += [seq(mk), mqt, mkt, mvt, seq(gi), seq(gf)]
    has_init = init is not None
    if has_init:
        in_specs += [
            pl.BlockSpec((nb, None, 2, ML_HEADS, ML_DK, ML_DK), lambda b, c: (b, l, 0, 0, 0, 0)),
            pl.BlockSpec((nb, None, 2, ML_HEADS, ML_DK), lambda b, c: (b, l, 0, 0, 0)),
            pl.BlockSpec((nb, None, 2, 1, LANES), lambda b, c: (b, l, 0, 0, 0)),
        ]
        args += list(init)
    out_specs = [pl.BlockSpec((nb, chunk, ML_W), lambda b, c: (b, c, 0)),
                 pl.BlockSpec((nb, chunk, ML_W), lambda b, c: (b, nc - 1 - c, 0))]
    out_shape = [jax.ShapeDtypeStruct((batch, t, ML_W), F32)] * 2
    if want_state:
        out_specs += [
            pl.BlockSpec((nb, 2, ML_HEADS, ML_DK, ML_DK), lambda b, c: (b, 0, 0, 0, 0)),
            pl.BlockSpec((nb, 2, ML_HEADS, ML_DK), lambda b, c: (b, 0, 0, 0)),
            pl.BlockSpec((nb, 2, 1, LANES), lambda b, c: (b, 0, 0, 0)),
        ]
        out_shape += [jax.ShapeDtypeStruct((batch, 2, ML_HEADS, ML_DK, ML_DK), F32),
                      jax.ShapeDtypeStruct((batch, 2, ML_HEADS, ML_DK), F32),
                      jax.ShapeDtypeStruct((batch, 2, 1, LANES), F32)]
    outs = pl.pallas_call(
        functools.partial(_mlstm_kernel, chunk=chunk, nc=nc, nb=nb, has_init=has_init, want_state=want_state),
        grid=(batch // nb, nc),
        in_specs=in_specs,
        out_specs=out_specs,
        out_shape=out_shape,
        scratch_shapes=[pltpu.VMEM((nb, 2, ML_HEADS, ML_DK, LANES), F32), pltpu.VMEM((nb, 2, ML_HEADS, 1, LANES), F32),
                        pltpu.VMEM((nb, 2, 1, LANES), F32)],
        compiler_params=_params(("parallel", "arbitrary")),
        name="mlstm",
    )(*args)
    return [outs[0].reshape(batch * t, ML_W), outs[1].reshape(batch * t, ML_W)] + list(outs[2:])


def _outproj_kernel(x_ref, mod_ref, go_ref, hf_ref, hb_ref, mo_ref, lo_ref, onw_ref, seg_ref, w_ref, o_ref):
    gqa_w = GQA_HEADS * HEAD_DIM
    hsum = hf_ref[...] + hb_ref[...]
    hi, mid, lo = _split3(hsum * hsum)
    seg = seg_ref[...]
    ms = _dot(hi, seg) + _dot(mid, seg) + _dot(lo, seg)
    hn = hsum * lax.rsqrt(ms + EPS) * onw_ref[...]
    ml = (hn * jax.nn.sigmoid(mo_ref[...])).astype(BF16)
    out = (_dot(go_ref[...], w_ref[0:gqa_w, :]) + _dot(lo_ref[...], w_ref[gqa_w + ML_W:, :])
           + _dot(ml, w_ref[gqa_w:gqa_w + ML_W, :]))
    o_ref[...] = x_ref[...] + mod_ref[5:6, :] * out


def _outproj(x, mods, gqa_o, hf, hb, mo, mla_o, onw, seg, w_out, l, group_of_tile, tm):
    n, d = x.shape
    row = lambda i: (i, 0)
    lay = lambda shape: pl.BlockSpec((None,) + shape, lambda i: (l,) + (0,) * len(shape))
    return pl.pallas_call(
        _outproj_kernel,
        grid=(n // tm,),
        in_specs=[
            pl.BlockSpec((tm, d), row),
            pl.BlockSpec((None, None, N_MOD, d), lambda i: (l, group_of_tile(i), 0, 0)),
            pl.BlockSpec((tm, GQA_HEADS * HEAD_DIM), row),
            pl.BlockSpec((tm, ML_W), row), pl.BlockSpec((tm, ML_W), row), pl.BlockSpec((tm, ML_W), row),
            pl.BlockSpec((tm, MLA_HEADS * MLA_V), row),
            lay((1, ML_W)),
            pl.BlockSpec((ML_W, ML_W), lambda i: (0, 0)),
            lay((w_out.shape[1], d)),
        ],
        out_specs=pl.BlockSpec((tm, d), row),
        out_shape=jax.ShapeDtypeStruct((n, d), F32),
        compiler_params=_params(("parallel",)),
        name="outproj",
    )(x, mods, gqa_o, hf, hb, mo, mla_o, onw, seg, w_out)


def _pad_last(a, width):
    return jnp.pad(a, [(0, 0)] * (a.ndim - 1) + [(0, width - a.shape[-1])])


def _pad_heads(a, heads, width):
    lead = a.shape[:-1]
    return _pad_last(a.reshape(lead + (heads, width)), LANES).reshape(lead + (heads * LANES,))


def _axial_rope(seq, rot_dim):
    half = rot_dim // 2
    rows = seq // GRID_W
    freqs = ROPE_BASE ** (-jnp.arange(0, half, 2, dtype=F32) / half)
    r = jnp.repeat(jnp.arange(rows, dtype=F32), GRID_W)
    c = jnp.tile(jnp.arange(GRID_W, dtype=F32), rows)
    ang = jnp.concatenate([r[:, None] * freqs, c[:, None] * freqs], axis=-1)
    return jnp.cos(ang), jnp.sin(ang)


def _rope_tables(seq):
    c, s = _axial_rope(seq, HEAD_DIM)
    ones = jnp.ones((seq, LANES - HEAD_DIM), F32)
    cq = jnp.concatenate([c, c, ones], axis=-1)
    sq = jnp.concatenate([-s, s, 0.0 * ones], axis=-1)
    c, s = _axial_rope(seq, MLA_ROPE)
    one_a = jnp.ones((seq, MLA_NOPE), F32)
    one_b = jnp.ones((seq, LANES - MLA_QK), F32)
    cm = jnp.concatenate([one_a, c, c, one_b], axis=-1)
    sm = jnp.concatenate([0.0 * one_a, -s, s, 0.0 * one_b], axis=-1)
    return cq, sq, cm, sm


def _pick_tile(n, pref):
    t = min(n, pref)
    while n % t:
        t //= 2
    return t


class _Tiles(NamedTuple):
    rows_c: int
    rows_l: int
    rows_ci: int
    tq_c: int
    tq_l: int
    nseq_c: int
    chunk: int
    nb_c: int
    nb_l: int


def _plan_tiles(bc, tc, bl, tl):
    assert tc % LANES == 0 and tl % LANES == 0
    rows_ci = _pick_tile(bc * tc, ROW_TILE)
    assert rows_ci % tc == 0 or tc % rows_ci == 0
    tq_c = _pick_tile(tc, ROW_TILE)
    return _Tiles(rows_c=_pick_tile(bc * tc, ROW_TILE), rows_l=_pick_tile(tl, ROW_TILE), rows_ci=rows_ci,
                  tq_c=tq_c, tq_l=_pick_tile(tl, ROW_TILE), nseq_c=_pick_tile(bc, SEQS_PER_STEP) if tq_c == tc else 1,
                  chunk=LANES, nb_c=_pick_tile(bc, SEQS_PER_STEP), nb_l=_pick_tile(bl, 2))


def kernel(x_prompt, x_sample, c, cache_gqa_k, cache_gqa_v, cache_mla_ckv, cache_mla_krope, state_mlstm_C, state_mlstm_n, state_mlstm_m, c_ctx, w_ada, b_ada, norm_w, ffn_w_gate, ffn_w_up, ffn_w_down, w_in, gqa_q_norm, gqa_k_norm, mlstm_gate_b, mlstm_out_norm, mla_q_norm, mla_w_uq, mla_kv_norm, mla_w_ukv, w_out, final_norm):
    bc, tc, d = x_prompt.shape
    bl, tl, _ = x_sample.shape
    depth = w_ada.shape[0]
    past = cache_gqa_k.shape[2]
    assert 1 + bl <= COND_ROWS

    cond = jnp.concatenate([c_ctx[None, :], c, jnp.zeros((COND_ROWS - 1 - bl, d), F32)], axis=0)
    mods = _ada(cond, w_ada, b_ada).reshape(depth, COND_ROWS, N_MOD, d)

    wg, wu, wd = ffn_w_gate.astype(BF16), ffn_w_up.astype(BF16), ffn_w_down.astype(BF16)
    idx = [0]
    for wdt in (GQA_HEADS * HEAD_DIM, GQA_KV_HEADS * HEAD_DIM, GQA_KV_HEADS * HEAD_DIM, ML_W, ML_W, ML_W, ML_W,
                4 * ML_HEADS, MLA_RANK, MLA_RANK, MLA_ROPE):
        idx.append(idx[-1] + wdt)
    seg = [w_in[..., a:b] for a, b in zip(idx[:-1], idx[1:])]
    s_gq, s_gk, s_gv, s_mq, s_mk, s_mv, s_mo, s_mg, s_ql, s_kv, s_kr = seg
    kr_slot_w = jnp.pad(s_kr, [(0, 0), (0, 0), (MLA_NOPE, LANES - MLA_QK)])
    nh = ML_HEADS
    gi_slot_w = _pad_last(jnp.concatenate([s_mg[..., 0:nh], s_mg[..., 2 * nh:3 * nh]], axis=-1), LANES)
    gf_slot_w = _pad_last(jnp.concatenate([s_mg[..., nh:2 * nh], s_mg[..., 3 * nh:4 * nh]], axis=-1), LANES)
    w_in_p = jnp.concatenate(
        [_pad_heads(s_gq, GQA_HEADS, HEAD_DIM), _pad_heads(s_gk, GQA_KV_HEADS, HEAD_DIM), s_gv,
         s_mk, s_mo, s_ql, s_kv, kr_slot_w, gi_slot_w, gf_slot_w], axis=-1).astype(BF16)
    assert w_in_p.shape[-1] == D_INP
    wmt = jnp.swapaxes(jnp.concatenate([s_mq, s_mk, s_mv], axis=-1), 1, 2).astype(BF16)
    wq = _pad_heads(mla_w_uq, MLA_HEADS, MLA_QK).astype(BF16)
    ukv = mla_w_ukv.reshape(depth, MLA_RANK, MLA_HEADS, MLA_NOPE + MLA_V)
    wk = _pad_last(ukv[..., :MLA_NOPE], LANES).reshape(depth, MLA_RANK, MLA_HEADS * LANES).astype(BF16)
    wvt = jnp.swapaxes(ukv[..., MLA_NOPE:].reshape(depth, MLA_RANK, MLA_HEADS * MLA_V), 1, 2).astype(BF16)
    wgvt = jnp.swapaxes(s_gv, 1, 2).astype(BF16)
    w_out_b = w_out.astype(BF16)
    qn = _pad_last(gqa_q_norm, LANES).reshape(depth, 1, LANES)
    kn = _pad_last(gqa_k_norm, LANES).reshape(depth, 1, LANES)
    gbi = _pad_last(jnp.concatenate([mlstm_gate_b[:, 0], mlstm_gate_b[:, 2]], axis=-1), LANES).reshape(depth, 1, LANES)
    gbf = _pad_last(jnp.concatenate([mlstm_gate_b[:, 1], mlstm_gate_b[:, 3]], axis=-1), LANES).reshape(depth, 1, LANES)
    mqn = mla_q_norm.reshape(depth, 1, MLA_RANK)
    mkvn = mla_kv_norm.reshape(depth, 1, MLA_RANK)
    onw = mlstm_out_norm.reshape(depth, 1, ML_W)
    norm_w4 = norm_w.reshape(depth, 3, 1, d)
    head_id = jnp.arange(ML_W) // ML_DK
    seg_mean = ((head_id[:, None] == head_id[None, :]).astype(F32) / ML_DK).astype(BF16)
    rope_tabs = _rope_tables(tl)

    ck_gqa = _pad_heads(cache_gqa_k.reshape(bl, depth, past, GQA_KV_HEADS * HEAD_DIM), GQA_KV_HEADS,
                        HEAD_DIM).astype(BF16)
    cvt_gqa = jnp.transpose(cache_gqa_v.reshape(bl * depth, past, GQA_KV_HEADS * HEAD_DIM), (2, 0, 1)).reshape(
        GQA_KV_HEADS * HEAD_DIM, bl, depth, past).astype(BF16)
    ckr_slot = jnp.pad(cache_mla_krope, [(0, 0), (0, 0), (0, 0), (MLA_NOPE, LANES - MLA_QK)])
    m0_lanes = jnp.stack([jnp.pad(state_mlstm_m[:, :, dd], [(0, 0), (0, 0), (dd * nh, LANES - (dd + 1) * nh)])
                          for dd in range(2)], axis=2)[:, :, :, None, :]

    nc_rows, nl_rows = bc * tc, bl * tl
    tiles = _plan_tiles(bc, tc, bl, tl)
    tm_c, tm_l, tm_ci = tiles.rows_c, tiles.rows_l, tiles.rows_ci
    tq_c, tq_l, nseq_c = tiles.tq_c, tiles.tq_l, tiles.nseq_c
    chunk_c = chunk_l = tiles.chunk
    nb_c, nb_l = tiles.nb_c, tiles.nb_l
    ctx_group = lambda i: 0
    lat_group = lambda i: 1 + i // (tl // tm_l)

    xc = x_prompt.reshape(nc_rows, d)
    xl = x_sample.reshape(nl_rows, d)
    collected = [[] for _ in range(7)]
    for l in range(depth):
        xc = _ffn(xc, mods, norm_w4, wg, wu, wd, l, 0, ctx_group, tm_c)
        xl = _ffn(xl, mods, norm_w4, wg, wu, wd, l, 0, lat_group, tm_l)

        (gq, gk, gv, mk, mqt, mkt, mvt, mo, gi, gf, qm, kp, vm, gkf, gvf, ckv, krs) = _inproj(
            xc, mods, norm_w4, w_in_p, wgvt, wmt, qn, kn, gbi, gbf, mqn, mkvn, wq, wk, wvt, None, l, ctx_group, tm_ci,
            tc, True)
        gqa_o = _attn(gq, [gk], [gv], bc, GQA_HEADS, GQA_GROUP, tq_c, nseq_c)
        mla_o = _attn(qm, [kp], [vm], bc, MLA_HEADS, 1, tq_c, nseq_c)
        hf, hb, c_new, n_new, m_lanes = _mlstm(mk, mqt, mkt, mvt, gi, gf, bc, chunk_c, nb_c, None, l, True)
        m_new = jnp.stack([m_lanes[:, dd, 0, dd * nh:(dd + 1) * nh] for dd in range(2)], axis=1)
        xc = _outproj(xc, mods, gqa_o, hf, hb, mo, mla_o, onw, seg_mean, w_out_b, l, ctx_group, tm_c)
        new_k = gkf.reshape(bc, tc, GQA_KV_HEADS, HEAD_DIM)
        new_v = gvf.reshape(bc, tc, GQA_KV_HEADS, HEAD_DIM)
        new_ckv = ckv.reshape(bc, tc, MLA_RANK)
        new_kr = krs.reshape(bc, tc, LANES)[..., MLA_NOPE:MLA_QK]
        for lst, t in zip(collected, (new_k, new_v, new_ckv, new_kr, c_new, n_new, m_new)):
            lst.append(t)

        (gq, gk, gv, mk, mqt, mkt, mvt, mo, gi, gf, qm, kp, vm) = _inproj(
            xl, mods, norm_w4, w_in_p, wgvt, wmt, qn, kn, gbi, gbf, mqn, mkvn, wq, wk, wvt, rope_tabs, l, lat_group,
            tm_l, tl, False)
        kp_c, vm_c = _kvexp(cache_mla_ckv[:, l].reshape(bl * past, MLA_RANK), ckr_slot[:, l].reshape(bl * past, LANES),
                            wk, wvt, l)
        gqa_o = _attn(gq, [ck_gqa[:, l].reshape(bl * past, -1), gk], [cvt_gqa[:, :, l].reshape(-1, bl * past), gv],
                      bl, GQA_HEADS, GQA_GROUP, tq_l, 1)
        mla_o = _attn(qm, [kp_c, kp], [vm_c, vm], bl, MLA_HEADS, 1, tq_l, 1)
        hf, hb = _mlstm(mk, mqt, mkt, mvt, gi, gf, bl, chunk_l, nb_l, (state_mlstm_C, state_mlstm_n, m0_lanes), l,
                        False)
        xl = _outproj(xl, mods, gqa_o, hf, hb, mo, mla_o, onw, seg_mean, w_out_b, l, lat_group, tm_l)

        fw = final_norm if l == depth - 1 else None
        xc = _ffn(xc, mods, norm_w4, wg, wu, wd, l, 1, ctx_group, tm_c, fw)
        xl = _ffn(xl, mods, norm_w4, wg, wu, wd, l, 1, lat_group, tm_l, fw)

    y_prompt = xc.reshape(bc, tc, d)
    y_sample = xl.reshape(bl, tl, d)
    stacked = [jnp.stack(lst, axis=1) for lst in collected]
    return (y_prompt, y_sample, *stacked)
```

```python
import functools
from typing import NamedTuple

import jax
import jax.numpy as jnp
from jax import lax
from jax.experimental import pallas as pl
from jax.experimental.pallas import tpu as pltpu

F32 = jnp.float32
BF16 = jnp.bfloat16

EPS = 1e-6
ROPE_BASE = 10000.0
GRID_W = 64
N_MOD = 9
HEAD_DIM = 64
GQA_HEADS = 6
GQA_KV_HEADS = 2
GQA_GROUP = GQA_HEADS // GQA_KV_HEADS
ML_HEADS = 4
ML_DK = 64
ML_W = ML_HEADS * ML_DK
MLA_HEADS = 6
MLA_RANK = 256
MLA_NOPE = 64
MLA_ROPE = 32
MLA_V = 64
MLA_QK = MLA_NOPE + MLA_ROPE
LANES = 128
COND_ROWS = 8
VMEM_LIMIT = 48 * 1024 * 1024
FFN_VMEM_LIMIT = 56 * 1024 * 1024
FFN_BLOCKS = 2
LOG2E = 1.4426950408889634
ROW_TILE = 512
SEQS_PER_STEP = 4
REDUCE_ROWS = 64

O_GQ = 0
O_GK = O_GQ + GQA_HEADS * LANES
O_GV = O_GK + GQA_KV_HEADS * LANES
O_MK = O_GV + GQA_KV_HEADS * HEAD_DIM
O_MO = O_MK + ML_W
O_QL = O_MO + ML_W
O_KV = O_QL + MLA_RANK
O_KR = O_KV + MLA_RANK
O_GI = O_KR + LANES
O_GF = O_GI + LANES
D_INP = O_GF + LANES


def _params(sem):
    return pltpu.CompilerParams(dimension_semantics=sem, vmem_limit_bytes=VMEM_LIMIT)


def _norm_mod(x, nw, sc, sh):
    ms = jnp.mean(x * x, axis=-1, keepdims=True)
    return (x * lax.rsqrt(ms + EPS) * nw) * (1.0 + sc) + sh


def _dot(a, b):
    return jnp.dot(a, b, preferred_element_type=F32)


def _dot_nt(a, b):
    return lax.dot_general(a, b, (((1,), (1,)), ((), ())), preferred_element_type=F32)


def _split2(x):
    hi = x.astype(BF16)
    return hi, (x - hi.astype(F32)).astype(BF16)


def _split3(x):
    hi = x.astype(BF16)
    r1 = x - hi.astype(F32)
    mid = r1.astype(BF16)
    lo = (r1 - mid.astype(F32)).astype(BF16)
    return hi, mid, lo


def _dot_exact_lhs(a_bf16, x):
    hi, mid, lo = _split3(x)
    return _dot(a_bf16, hi) + _dot(a_bf16, mid) + _dot(a_bf16, lo)


def _ada_kernel(cond_ref, w_ref, b_ref, o_ref):
    c = cond_ref[...]
    s = (c * jax.nn.sigmoid(c)).astype(BF16)
    o_ref[...] = _dot(s, w_ref[...].astype(BF16)) + b_ref[...]


def _ada(cond, w_ada, b_ada):
    depth, d, nd = w_ada.shape
    tn = d
    return pl.pallas_call(
        _ada_kernel,
        grid=(depth, nd // tn),
        in_specs=[
            pl.BlockSpec((COND_ROWS, d), lambda l, j: (0, 0)),
            pl.BlockSpec((None, d, tn), lambda l, j: (l, 0, j)),
            pl.BlockSpec((None, 1, tn), lambda l, j: (l, 0, j)),
        ],
        out_specs=pl.BlockSpec((None, COND_ROWS, tn), lambda l, j: (l, 0, j)),
        out_shape=jax.ShapeDtypeStruct((depth, COND_ROWS, nd), F32),
        compiler_params=_params(("parallel", "parallel")),
        name="ada",
    )(cond, w_ada, b_ada.reshape(depth, 1, nd))


def _ffn_kernel(x_ref, mod_ref, nw_ref, wg_ref, wu_ref, wd_ref, *rest, mi):
    o_ref = rest[-1]
    rows_per_block = x_ref.shape[0] // FFN_BLOCKS
    for s in range(FFN_BLOCKS):
        rows = slice(s * rows_per_block, (s + 1) * rows_per_block)
        x = x_ref[rows, :]
        h = _norm_mod(x, nw_ref[...], mod_ref[mi + 1:mi + 2, :], mod_ref[mi:mi + 1, :]).astype(BF16)
        g = _dot(h, wg_ref[...])
        u = _dot(h, wu_ref[...])
        a = (g * jax.nn.sigmoid(g) * u).astype(BF16)
        y = x + (0.5 * mod_ref[mi + 2:mi + 3, :]) * _dot(a, wd_ref[...])
        if len(rest) == 2:
            y = y * lax.rsqrt(jnp.mean(y * y, axis=-1, keepdims=True) + EPS) * rest[0][...]
        o_ref[rows, :] = y


def _ffn(x, mods, norm_w4, wg, wu, wd, l, j, group_of_tile, tm, final_w=None):
    n, d = x.shape
    dff = wg.shape[-1]
    mi = 0 if j == 0 else 6
    nwi = 0 if j == 0 else 2
    resident = lambda shape: pl.BlockSpec((None, None) + shape, lambda i: (l, j, 0, 0),
                                          pipeline_mode=pl.Buffered(1))
    return pl.pallas_call(
        functools.partial(_ffn_kernel, mi=mi),
        grid=(n // tm,),
        in_specs=[
            pl.BlockSpec((tm, d), lambda i: (i, 0)),
            pl.BlockSpec((None, None, N_MOD, d), lambda i: (l, group_of_tile(i), 0, 0)),
            pl.BlockSpec((None, None, 1, d), lambda i: (l, nwi, 0, 0)),
            resident((d, dff)), resident((d, dff)), resident((dff, d)),
        ] + ([] if final_w is None else [pl.BlockSpec((1, d), lambda i: (0, 0))]),
        out_specs=pl.BlockSpec((tm, d), lambda i: (i, 0)),
        out_shape=jax.ShapeDtypeStruct((n, d), F32),
        compiler_params=pltpu.CompilerParams(dimension_semantics=("parallel",), vmem_limit_bytes=FFN_VMEM_LIMIT),
        name="ffn",
    )(x, mods, norm_w4, wg, wu, wd, *([] if final_w is None else [final_w.reshape(1, d)]))


def _inproj_kernel(*refs, rope, ctx_out):
    it = iter(refs)
    (x_ref, mod_ref, nw_ref, w_ref, wgvt_ref, wmt_ref, qn_ref, kn_ref, gbi_ref, gbf_ref, mqn_ref, mkvn_ref, wq_ref,
     wk_ref, wvt_ref) = (next(it) for _ in range(15))
    if rope:
        cq_ref, sq_ref, cm_ref, sm_ref = (next(it) for _ in range(4))
    gq_o, gk_o, gv_o, mk_o, mqt_o, mkt_o, mvt_o, mo_o, gi_o, gf_o, qm_o, kp_o, vm_o = (next(it) for _ in range(13))
    if ctx_out:
        gkf_o, gvf_o, ckv_o, kr_o = (next(it) for _ in range(4))

    x = x_ref[...]
    tm = x.shape[0]
    hb = _norm_mod(x, nw_ref[...], mod_ref[4:5, :], mod_ref[3:4, :]).astype(BF16)

    def proj(o, n):
        return _dot(hb, w_ref[:, o:o + n])

    lane = lax.broadcasted_iota(jnp.int32, (tm, LANES), 1)

    def head_norm(xg, wrow):
        ms = jnp.sum(xg * xg, axis=-1, keepdims=True) * (1.0 / HEAD_DIM)
        return xg * lax.rsqrt(ms + EPS) * wrow

    def rope_hd(xg):
        half = HEAD_DIM // 2
        partner = jnp.where(lane < half, pltpu.roll(xg, LANES - half, 1), pltpu.roll(xg, half, 1))
        return xg * cq_ref[...] + partner * sq_ref[...]

    def rope_mla(xg):
        half = MLA_ROPE // 2
        partner = jnp.where((lane & half) == 0, pltpu.roll(xg, LANES - half, 1), pltpu.roll(xg, half, 1))
        return xg * cm_ref[...] + partner * sm_ref[...]

    gq = proj(O_GQ, GQA_HEADS * LANES)
    for g in range(GQA_HEADS):
        qg = head_norm(gq[:, g * LANES:(g + 1) * LANES], qn_ref[...])
        if rope:
            qg = rope_hd(qg)
        gq_o[:, g * LANES:(g + 1) * LANES] = (qg * (LOG2E * HEAD_DIM ** -0.5)).astype(BF16)
    gk = proj(O_GK, GQA_KV_HEADS * LANES)
    kg_ctx = []
    for g in range(GQA_KV_HEADS):
        kg = head_norm(gk[:, g * LANES:(g + 1) * LANES], kn_ref[...])
        if ctx_out:
            kg_ctx.append(kg)
        if rope:
            kg = rope_hd(kg)
        gk_o[:, g * LANES:(g + 1) * LANES] = kg.astype(BF16)
    gv_o[...] = _dot_nt(wgvt_ref[...], hb).astype(BF16)
    if ctx_out:
        gkf_o[...] = jnp.where(lane < HEAD_DIM, kg_ctx[0], pltpu.roll(kg_ctx[1], HEAD_DIM, 1))
        gvf_o[...] = proj(O_GV, GQA_KV_HEADS * HEAD_DIM)

    mk_o[...] = proj(O_MK, ML_W).astype(BF16)
    mt = _dot_nt(wmt_ref[...], hb)
    rs = mqt_o.shape[2]
    for s in range(mqt_o.shape[0]):
        cols = slice(s * rs, (s + 1) * rs)
        mqt_o[s] = (mt[0:ML_W, cols] * (ML_DK ** -0.5)).astype(BF16)
        mkt_o[s] = mt[ML_W:2 * ML_W, cols].astype(BF16)
        mvt_o[s] = mt[2 * ML_W:3 * ML_W, cols].astype(BF16)
    mo_o[...] = proj(O_MO, ML_W)
    gi_o[...] = proj(O_GI, LANES) + gbi_ref[...]
    gf_o[...] = proj(O_GF, LANES) + gbf_ref[...]

    xq = proj(O_QL, MLA_RANK)
    ms = jnp.mean(xq * xq, axis=-1, keepdims=True)
    qn = (xq * lax.rsqrt(ms + EPS) * mqn_ref[...]).astype(BF16)
    qm = _dot(qn, wq_ref[...])
    for g in range(MLA_HEADS):
        qg = qm[:, g * LANES:(g + 1) * LANES]
        if rope:
            qg = rope_mla(qg)
        qm_o[:, g * LANES:(g + 1) * LANES] = (qg * (LOG2E * MLA_QK ** -0.5)).astype(BF16)

    xkv = proj(O_KV, MLA_RANK)
    ms = jnp.mean(xkv * xkv, axis=-1, keepdims=True)
    ckv = xkv * lax.rsqrt(ms + EPS) * mkvn_ref[...]
    kr = proj(O_KR, LANES)
    if rope:
        kr = rope_mla(kr)
    if ctx_out:
        ckv_o[...] = ckv
        kr_o[...] = kr
    cb = ckv.astype(BF16)
    kk = _dot(cb, wk_ref[...])
    for g in range(MLA_HEADS):
        kp_o[:, g * LANES:(g + 1) * LANES] = (kk[:, g * LANES:(g + 1) * LANES] + kr).astype(BF16)
    vm_o[...] = _dot_nt(wvt_ref[...], cb).astype(BF16)


def _inproj(x, mods, norm_w4, w_in_p, wgvt, wmt, qn, kn, gbi, gbf, mqn, mkvn, wq, wk, wvt, rope_tabs, l, group_of_tile,
            tm, rows_per_seq, ctx_out):
    n, d = x.shape
    rope = rope_tabs is not None
    tiles_per_seq = max(1, rows_per_seq // tm)
    seqs_per_tile = max(1, tm // rows_per_seq)
    rows_in_tile = tm // seqs_per_tile
    row = lambda i: (i, 0)
    lay = lambda shape: pl.BlockSpec((None,) + shape, lambda i: (l,) + (0,) * len(shape))
    in_specs = [
        pl.BlockSpec((tm, d), row),
        pl.BlockSpec((None, None, N_MOD, d), lambda i: (l, group_of_tile(i), 0, 0)),
        pl.BlockSpec((None, None, 1, d), lambda i: (l, 1, 0, 0)),
        lay((d, D_INP)), lay((GQA_KV_HEADS * HEAD_DIM, d)), lay((3 * ML_W, d)),
        lay((1, LANES)), lay((1, LANES)), lay((1, LANES)), lay((1, LANES)), lay((1, MLA_RANK)), lay((1, MLA_RANK)),
        lay((MLA_RANK, MLA_HEADS * LANES)), lay((MLA_RANK, MLA_HEADS * LANES)), lay((MLA_HEADS * MLA_V, MLA_RANK)),
    ]
    args = [x, mods, norm_w4, w_in_p, wgvt, wmt, qn, kn, gbi, gbf, mqn, mkvn, wq, wk, wvt]
    if rope:
        tab = pl.BlockSpec((tm, LANES), lambda i: (i % tiles_per_seq, 0))
        in_specs += [tab] * 4
        args += list(rope_tabs)
    widths = [(GQA_HEADS * LANES, BF16, "row"), (GQA_KV_HEADS * LANES, BF16, "row"),
              (GQA_KV_HEADS * HEAD_DIM, BF16, "col"),
              (ML_W, BF16, "row"), (ML_W, BF16, "seq"), (ML_W, BF16, "seq"), (ML_W, BF16, "seq"),
              (ML_W, F32, "row"), (LANES, F32, "row"), (LANES, F32, "row"),
              (MLA_HEADS * LANES, BF16, "row"), (MLA_HEADS * LANES, BF16, "row"), (MLA_HEADS * MLA_V, BF16, "col")]
    if ctx_out:
        widths += [(GQA_KV_HEADS * HEAD_DIM, F32, "row"), (GQA_KV_HEADS * HEAD_DIM, F32, "row"),
                   (MLA_RANK, F32, "row"), (LANES, F32, "row")]
    spec = {"row": lambda w: pl.BlockSpec((tm, w), row),
            "col": lambda w: pl.BlockSpec((w, tm), lambda i: (0, i)),
            "seq": lambda w: pl.BlockSpec((seqs_per_tile, w, rows_in_tile),
                                          lambda i: (i // tiles_per_seq, 0, i % tiles_per_seq))}
    shape = {"row": lambda w: (n, w), "col": lambda w: (w, n), "seq": lambda w: (n // rows_per_seq, w, rows_per_seq)}
    return pl.pallas_call(
        functools.partial(_inproj_kernel, rope=rope, ctx_out=ctx_out),
        grid=(n // tm,),
        in_specs=in_specs,
        out_specs=[spec[kind](w) for w, _, kind in widths],
        out_shape=[jax.ShapeDtypeStruct(shape[kind](w), dt) for w, dt, kind in widths],
        compiler_params=_params(("parallel",)),
        name="inproj",
    )(*args)


def _kvexp_kernel(ckv_ref, kr_ref, wk_ref, wvt_ref, kp_o, vm_o):
    cb = ckv_ref[...].astype(BF16)
    kk = _dot(cb, wk_ref[...])
    kr = kr_ref[...]
    for g in range(MLA_HEADS):
        kp_o[:, g * LANES:(g + 1) * LANES] = (kk[:, g * LANES:(g + 1) * LANES] + kr).astype(BF16)
    vm_o[...] = _dot_nt(wvt_ref[...], cb).astype(BF16)


def _kvexp(ckv, kr_slot, wk, wvt, l):
    n = ckv.shape[0]
    lay = lambda shape: pl.BlockSpec((None,) + shape, lambda i: (l,) + (0,) * len(shape))
    return pl.pallas_call(
        _kvexp_kernel,
        grid=(1,),
        in_specs=[pl.BlockSpec((n, MLA_RANK), lambda i: (0, 0)), pl.BlockSpec((n, LANES), lambda i: (0, 0)),
                  lay((MLA_RANK, MLA_HEADS * LANES)), lay((MLA_HEADS * MLA_V, MLA_RANK))],
        out_specs=[pl.BlockSpec((n, MLA_HEADS * LANES), lambda i: (0, 0)),
                   pl.BlockSpec((MLA_HEADS * MLA_V, n), lambda i: (0, 0))],
        out_shape=[jax.ShapeDtypeStruct((n, MLA_HEADS * LANES), BF16),
                   jax.ShapeDtypeStruct((MLA_HEADS * MLA_V, n), BF16)],
        compiler_params=_params(("arbitrary",)),
        name="kvexp",
    )(ckv, kr_slot, wk, wvt)


def _attn_kernel(*refs, n_src, heads, group, nseq, tq):
    q_ref = refs[0]
    k_refs = refs[1:1 + n_src]
    vt_refs = refs[1 + n_src:1 + 2 * n_src]
    o_ref = refs[1 + 2 * n_src]
    seq_rows = [k.shape[0] // nseq for k in k_refs]

    def reduce_rows(x, op):
        r = x.shape[0]
        part = op(x.reshape(r // REDUCE_ROWS, REDUCE_ROWS, x.shape[1]), axis=0) if r > REDUCE_ROWS else x
        return op(part, axis=0, keepdims=True)

    def scores(j, h):
        kvh = h // group
        qh = q_ref[j * tq:(j + 1) * tq, h * LANES:(h + 1) * LANES]
        return [_dot_nt(k[j * s:(j + 1) * s, kvh * LANES:(kvh + 1) * LANES], qh)
                for k, s in zip(k_refs, seq_rows)]

    units = [(j, h) for j in range(nseq) for h in range(heads)]
    sts_next = scores(*units[0])
    head_outs = []
    for idx, (j, h) in enumerate(units):
        kvh = h // group
        sts = sts_next
        if idx + 1 < len(units):
            sts_next = scores(*units[idx + 1])
        m = reduce_rows(sts[0], jnp.max)
        for st in sts[1:]:
            m = jnp.maximum(m, reduce_rows(st, jnp.max))
        den = None
        acc = None
        for st, vt, s in zip(sts, vt_refs, seq_rows):
            p = jnp.exp2(st - m)
            ps = reduce_rows(p, jnp.sum)
            pv = _dot(vt[kvh * MLA_V:(kvh + 1) * MLA_V, j * s:(j + 1) * s], p.astype(BF16))
            den = ps if den is None else den + ps
            acc = pv if acc is None else acc + pv
        head_outs.append(acc / den)
        if h == heads - 1:
            pairs = [jnp.concatenate(head_outs[i:i + 2], axis=0).T for i in range(0, heads, 2)]
            o_ref[j * tq:(j + 1) * tq, :] = jnp.concatenate(pairs, axis=-1).astype(BF16)
            head_outs = []


def _attn(q, ks, vts, batch, heads, group, tq, nseq):
    n = q.shape[0]
    t = n // batch
    nq = t // tq
    assert nseq == 1 or nq == 1
    n_src = len(ks)
    in_specs = [pl.BlockSpec((nseq * tq, heads * LANES), lambda b, i: (b * nq + i, 0))]
    for a in ks:
        in_specs.append(pl.BlockSpec((nseq * a.shape[0] // batch, a.shape[1]), lambda b, i: (b, 0)))
    for a in vts:
        in_specs.append(pl.BlockSpec((a.shape[0], nseq * a.shape[1] // batch), lambda b, i: (0, b)))
    return pl.pallas_call(
        functools.partial(_attn_kernel, n_src=n_src, heads=heads, group=group, nseq=nseq, tq=tq),
        grid=(batch // nseq, nq),
        in_specs=in_specs,
        out_specs=pl.BlockSpec((nseq * tq, heads * MLA_V), lambda b, i: (b * nq + i, 0)),
        out_shape=jax.ShapeDtypeStruct((n, heads * MLA_V), BF16),
        compiler_params=_params(("parallel", "parallel")),
        name="attn",
    )(q, *ks, *vts)


def _mxu_transpose(eye, parts):
    out = _dot_nt(eye, parts[0])
    for p in parts[1:]:
        out = out + _dot_nt(eye, p)
    return out


def _eye(n, m):
    return (lax.broadcasted_iota(jnp.int32, (n, m), 0) == lax.broadcasted_iota(jnp.int32, (n, m), 1)).astype(BF16)


def _mlstm_kernel(*refs, chunk, nc, nb, has_init, want_state):
    it = iter(refs)
    fwd = tuple(next(it) for _ in range(6))
    bwd = tuple(next(it) for _ in range(6))
    if has_init:
        c0_ref, n0_ref, m0_ref = (next(it) for _ in range(3))
    h_outs = (next(it), next(it))
    if want_state:
        c_out, n_out, m_out = (next(it) for _ in range(3))
    ct_s, n_s, m_s = (next(it) for _ in range(3))
    step = pl.program_id(1)
    L = chunk
    eye_l = _eye(L, L)
    eye_lanes = _eye(LANES, LANES)
    eye_dk = _eye(ML_DK, ML_DK)
    units = [(j, d, h) for j in range(nb) for d in range(2) for h in range(ML_HEADS)]

    @pl.when(step == 0)
    def _():
        if has_init:
            zpad = jnp.zeros((ML_DK, ML_DK), F32)
            for j, d, h in units:
                c0 = jnp.concatenate([c0_ref[j, d, h], zpad], axis=0)
                ct_s[j, d, h] = _mxu_transpose(eye_dk, _split3(c0))
                n_s[j, d, h] = jnp.concatenate([n0_ref[j, d, h:h + 1, :], zpad[0:1]], axis=1)
            m_s[...] = m0_ref[...]
        else:
            ct_s[...] = jnp.zeros_like(ct_s)
            n_s[...] = jnp.zeros_like(n_s)
            m_s[...] = jnp.zeros_like(m_s)

    prev = {u: (ct_s[u[0], u[1], u[2]], n_s[u[0], u[1], u[2]]) for u in units}
    m_prev = {(j, d): m_s[j, d] for j in range(nb) for d in range(2)}
    new, m_next = {}, {}

    row = lax.broadcasted_iota(jnp.int32, (L, L), 0)
    col = lax.broadcasted_iota(jnp.int32, (L, L), 1)
    sel_row = lax.broadcasted_iota(jnp.int32, (LANES, L), 0)
    head_of_lane = lax.broadcasted_iota(jnp.int32, (L, ML_W), 1) // ML_DK
    ones_rows = jnp.ones((2 * COND_ROWS, L), BF16)
    zero_rows = jnp.zeros((2 * COND_ROWS, L), BF16)
    zero_half = jnp.zeros((ML_DK, L), BF16)
    seq_dirs = [(j, d) for j in range(nb) for d in range(2)]
    blocks = {jd: (fwd if jd[1] == 0 else bwd) for jd in seq_dirs}
    tri = [(col <= row).astype(BF16), (col >= row).astype(BF16)]
    valid_t = [row <= col, row >= col]

    bcum = {(j, d): _dot_exact_lhs(tri[d], jax.nn.log_sigmoid(blocks[(j, d)][5][j])) for j, d in seq_dirs}

    inter_t, w_t, a_parts = {}, {}, {}
    for j, d in seq_dirs:
        a = blocks[(j, d)][4][j] - bcum[(j, d)]
        m_vec = m_prev[(j, d)]
        b_last = bcum[(j, d)][L - 1:L, :] if d == 0 else bcum[(j, d)][0:1, :]
        mx = jnp.maximum(m_vec, jnp.max(a, axis=0, keepdims=True))
        m_next[(j, d)] = b_last + mx
        w_old = jnp.exp(m_vec - mx)
        inter_t[(j, d)] = _mxu_transpose(eye_lanes, _split2(bcum[(j, d)] + m_vec))
        w_t[(j, d)] = _mxu_transpose(eye_lanes, _split2(jnp.exp(a - mx)))
        a_ext = jnp.concatenate([a - m_vec, jnp.broadcast_to(w_old, (2 * COND_ROWS, LANES))], axis=0)
        a_parts[(j, d)] = _split2(a_ext)

    abc, st = {}, {}
    for j, d, h in units:
        pick = (sel_row == d * ML_HEADS + h).astype(BF16)
        parts = a_parts[(j, d)]
        abc[(j, d, h)] = _dot(parts[0], pick) + _dot(parts[1], pick)
        k_tok = blocks[(j, d)][0][j]
        st[(j, d, h)] = _dot(jnp.where(head_of_lane == h, k_tok, jnp.zeros_like(k_tok)), blocks[(j, d)][1][j])

    upd = {}
    for j, d, h in units:
        u = d * ML_HEADS + h
        kt_h = blocks[(j, d)][2][j, h * ML_DK:(h + 1) * ML_DK, :]
        v_h = blocks[(j, d)][3][j, h * ML_DK:(h + 1) * ML_DK, :]
        kw = (kt_h.astype(F32) * w_t[(j, d)][u:u + 1, :]).astype(BF16)
        upd[(j, d, h)] = _dot_nt(jnp.concatenate([v_h, ones_rows], axis=0),
                                 jnp.concatenate([kw, zero_half], axis=0))

    hts = {}
    for j, d, h in units:
        u = d * ML_HEADS + h
        ct, n_row = prev[(j, d, h)]
        w_old_u = abc[(j, d, h)][L:L + 1, :]
        dm = jnp.where(valid_t[d], abc[(j, d, h)][0:L], -jnp.inf)
        mp = jnp.maximum(jnp.max(dm, axis=0, keepdims=True), 0.0)
        w_inter = jnp.exp(-mp)
        qk_t = st[(j, d, h)] * jnp.exp(dm - mp)
        q_h = blocks[(j, d)][1][j, h * ML_DK:(h + 1) * ML_DK, :]
        v_h = blocks[(j, d)][3][j, h * ML_DK:(h + 1) * ML_DK, :]
        rhs = jnp.concatenate([qk_t.astype(BF16), (q_h.astype(F32) * w_inter).astype(BF16), zero_half], axis=0)
        lhs = jnp.concatenate([
            jnp.concatenate([v_h, ct.astype(BF16)], axis=1),
            jnp.concatenate([zero_rows, jnp.broadcast_to(n_row, (2 * COND_ROWS, LANES)).astype(BF16)], axis=1),
        ], axis=0)
        res = _dot(lhs, rhs)
        den = res[ML_DK:ML_DK + 1] + jnp.sum(qk_t, axis=0, keepdims=True)
        hts[(j, d, h)] = res[0:ML_DK] / jnp.maximum(jnp.abs(den), jnp.exp(-(inter_t[(j, d)][u:u + 1, :] + mp)))
        new[(j, d, h)] = (w_old_u * ct + upd[(j, d, h)][0:ML_DK],
                          w_old_u * n_row + upd[(j, d, h)][ML_DK:ML_DK + 1])

    for j, d in seq_dirs:
        ht_all = jnp.concatenate([hts[(j, d, h)] for h in range(ML_HEADS)], axis=0)
        h_outs[d][j] = _mxu_transpose(eye_l, _split2(ht_all))


    for (j, d, h), (ct_new, n_new) in new.items():
        ct_s[j, d, h] = ct_new
        n_s[j, d, h] = n_new
    for (j, d), m_new in m_next.items():
        m_s[j, d] = m_new

    if want_state:
        @pl.when(step == nc - 1)
        def _():
            pad_eye = _eye(ML_DK, LANES)
            for j, d, h in units:
                c_out[j, d, h] = _mxu_transpose(pad_eye, _split3(new[(j, d, h)][0]))
                n_out[j, d, h:h + 1, :] = new[(j, d, h)][1][:, 0:ML_DK]
            for (j, d), m_new in m_next.items():
                m_out[j, d] = m_new


def _mlstm(mk, mqt, mkt, mvt, gi, gf, batch, chunk, nb, init, l, want_state):
    t = mk.shape[0] // batch
    nc = t // chunk
    seq = lambda a: a.reshape(batch, t, a.shape[-1])
    in_specs, args = [], []
    for cidx in (lambda c: c, lambda c: nc - 1 - c):
        tok = lambda b, c, cidx=cidx: (b, cidx(c), 0)
        tr = lambda b, c, cidx=cidx: (b, 0, cidx(c))
        in_specs += [pl.BlockSpec((nb, chunk, ML_W), tok)] + [pl.BlockSpec((nb, ML_W, chunk), tr)] * 3
        in_specs += [pl.BlockSpec((nb, chunk, LANES), tok)] * 2
        args += [seq(mk), mqt, mkt, mvt, seq(gi), seq(gf)]
    has_init = init is not None
    if has_init:
        in_specs += [
            pl.BlockSpec((nb, None, 2, ML_HEADS, ML_DK, ML_DK), lambda b, c: (b, l, 0, 0, 0, 0)),
            pl.BlockSpec((nb, None, 2, ML_HEADS, ML_DK), lambda b, c: (b, l, 0, 0, 0)),
            pl.BlockSpec((nb, None, 2, 1, LANES), lambda b, c: (b, l, 0, 0, 0)),
        ]
        args += list(init)
    out_specs = [pl.BlockSpec((nb, chunk, ML_W), lambda b, c: (b, c, 0)),
                 pl.BlockSpec((nb, chunk, ML_W), lambda b, c: (b, nc - 1 - c, 0))]
    out_shape = [jax.ShapeDtypeStruct((batch, t, ML_W), F32)] * 2
    if want_state:
        out_specs += [
            pl.BlockSpec((nb, 2, ML_HEADS, ML_DK, ML_DK), lambda b, c: (b, 0, 0, 0, 0)),
            pl.BlockSpec((nb, 2, ML_HEADS, ML_DK), lambda b, c: (b, 0, 0, 0)),
            pl.BlockSpec((nb, 2, 1, LANES), lambda b, c: (b, 0, 0, 0)),
        ]
        out_shape += [jax.ShapeDtypeStruct((batch, 2, ML_HEADS, ML_DK, ML_DK), F32),
                      jax.ShapeDtypeStruct((batch, 2, ML_HEADS, ML_DK), F32),
                      jax.ShapeDtypeStruct((batch, 2, 1, LANES), F32)]
    outs = pl.pallas_call(
        functools.partial(_mlstm_kernel, chunk=chunk, nc=nc, nb=nb, has_init=has_init, want_state=want_state),
        grid=(batch // nb, nc),
        in_specs=in_specs,
        out_specs=out_specs,
        out_shape=out_shape,
        scratch_shapes=[pltpu.VMEM((nb, 2, ML_HEADS, ML_DK, LANES), F32), pltpu.VMEM((nb, 2, ML_HEADS, 1, LANES), F32),
                        pltpu.VMEM((nb, 2, 1, LANES), F32)],
        compiler_params=_params(("parallel", "arbitrary")),
        name="mlstm",
    )(*args)
    return [outs[0].reshape(batch * t, ML_W), outs[1].reshape(batch * t, ML_W)] + list(outs[2:])


def _outproj_kernel(x_ref, mod_ref, go_ref, hf_ref, hb_ref, mo_ref, lo_ref, onw_ref, seg_ref, w_ref, o_ref):
    gqa_w = GQA_HEADS * HEAD_DIM
    hsum = hf_ref[...] + hb_ref[...]
    hi, mid, lo = _split3(hsum * hsum)
    seg = seg_ref[...]
    ms = _dot(hi, seg) + _dot(mid, seg) + _dot(lo, seg)
    hn = hsum * lax.rsqrt(ms + EPS) * onw_ref[...]
    ml = (hn * jax.nn.sigmoid(mo_ref[...])).astype(BF16)
    out = (_dot(go_ref[...], w_ref[0:gqa_w, :]) + _dot(lo_ref[...], w_ref[gqa_w + ML_W:, :])
           + _dot(ml, w_ref[gqa_w:gqa_w + ML_W, :]))
    o_ref[...] = x_ref[...] + mod_ref[5:6, :] * out


def _outproj(x, mods, gqa_o, hf, hb, mo, mla_o, onw, seg, w_out, l, group_of_tile, tm):
    n, d = x.shape
    row = lambda i: (i, 0)
    lay = lambda shape: pl.BlockSpec((None,) + shape, lambda i: (l,) + (0,) * len(shape))
    return pl.pallas_call(
        _outproj_kernel,
        grid=(n // tm,),
        in_specs=[
            pl.BlockSpec((tm, d), row),
            pl.BlockSpec((None, None, N_MOD, d), lambda i: (l, group_of_tile(i), 0, 0)),
            pl.BlockSpec((tm, GQA_HEADS * HEAD_DIM), row),
            pl.BlockSpec((tm, ML_W), row), pl.BlockSpec((tm, ML_W), row), pl.BlockSpec((tm, ML_W), row),
            pl.BlockSpec((tm, MLA_HEADS * MLA_V), row),
            lay((1, ML_W)),
            pl.BlockSpec((ML_W, ML_W), lambda i: (0, 0)),
            lay((w_out.shape[1], d)),
        ],
        out_specs=pl.BlockSpec((tm, d), row),
        out_shape=jax.ShapeDtypeStruct((n, d), F32),
        compiler_params=_params(("parallel",)),
        name="outproj",
    )(x, mods, gqa_o, hf, hb, mo, mla_o, onw, seg, w_out)


def _pad_last(a, width):
    return jnp.pad(a, [(0, 0)] * (a.ndim - 1) + [(0, width - a.shape[-1])])


def _pad_heads(a, heads, width):
    lead = a.shape[:-1]
    return _pad_last(a.reshape(lead + (heads, width)), LANES).reshape(lead + (heads * LANES,))


def _axial_rope(seq, rot_dim):
    half = rot_dim // 2
    rows = seq // GRID_W
    freqs = ROPE_BASE ** (-jnp.arange(0, half, 2, dtype=F32) / half)
    r = jnp.repeat(jnp.arange(rows, dtype=F32), GRID_W)
    c = jnp.tile(jnp.arange(GRID_W, dtype=F32), rows)
    ang = jnp.concatenate([r[:, None] * freqs, c[:, None] * freqs], axis=-1)
    return jnp.cos(ang), jnp.sin(ang)


def _rope_tables(seq):
    c, s = _axial_rope(seq, HEAD_DIM)
    ones = jnp.ones((seq, LANES - HEAD_DIM), F32)
    cq = jnp.concatenate([c, c, ones], axis=-1)
    sq = jnp.concatenate([-s, s, 0.0 * ones], axis=-1)
    c, s = _axial_rope(seq, MLA_ROPE)
    one_a = jnp.ones((seq, MLA_NOPE), F32)
    one_b = jnp.ones((seq, LANES - MLA_QK), F32)
    cm = jnp.concatenate([one_a, c, c, one_b], axis=-1)
    sm = jnp.concatenate([0.0 * one_a, -s, s, 0.0 * one_b], axis=-1)
    return cq, sq, cm, sm


def _pick_tile(n, pref):
    t = min(n, pref)
    while n % t:
        t //= 2
    return t


class _Tiles(NamedTuple):
    rows_c: int
    rows_l: int
    rows_ci: int
    ffn_c: int
    ffn_l: int
    tq_c: int
    tq_l: int
    nseq_c: int
    chunk: int
    nb_c: int
    nb_l: int


def _plan_tiles(bc, tc, bl, tl):
    assert tc % LANES == 0 and tl % LANES == 0
    rows_ci = _pick_tile(bc * tc, ROW_TILE)
    assert rows_ci % tc == 0 or tc % rows_ci == 0
    tq_c = _pick_tile(tc, ROW_TILE)
    return _Tiles(rows_c=_pick_tile(bc * tc, ROW_TILE), rows_l=_pick_tile(tl, ROW_TILE), rows_ci=rows_ci,
                  ffn_c=_pick_tile(bc * tc, FFN_BLOCKS * ROW_TILE), ffn_l=_pick_tile(tl, FFN_BLOCKS * ROW_TILE),
                  tq_c=tq_c, tq_l=_pick_tile(tl, ROW_TILE), nseq_c=_pick_tile(bc, SEQS_PER_STEP) if tq_c == tc else 1,
                  chunk=LANES, nb_c=_pick_tile(bc, 2 * SEQS_PER_STEP), nb_l=_pick_tile(bl, 2))


def kernel(x_prompt, x_sample, c, cache_gqa_k, cache_gqa_v, cache_mla_ckv, cache_mla_krope, state_mlstm_C, state_mlstm_n, state_mlstm_m, c_ctx, w_ada, b_ada, norm_w, ffn_w_gate, ffn_w_up, ffn_w_down, w_in, gqa_q_norm, gqa_k_norm, mlstm_gate_b, mlstm_out_norm, mla_q_norm, mla_w_uq, mla_kv_norm, mla_w_ukv, w_out, final_norm):
    bc, tc, d = x_prompt.shape
    bl, tl, _ = x_sample.shape
    depth = w_ada.shape[0]
    past = cache_gqa_k.shape[2]
    assert 1 + bl <= COND_ROWS

    cond = jnp.concatenate([c_ctx[None, :], c, jnp.zeros((COND_ROWS - 1 - bl, d), F32)], axis=0)
    mods = _ada(cond, w_ada, b_ada).reshape(depth, COND_ROWS, N_MOD, d)

    wg, wu, wd = ffn_w_gate.astype(BF16), ffn_w_up.astype(BF16), ffn_w_down.astype(BF16)
    idx = [0]
    for wdt in (GQA_HEADS * HEAD_DIM, GQA_KV_HEADS * HEAD_DIM, GQA_KV_HEADS * HEAD_DIM, ML_W, ML_W, ML_W, ML_W,
                4 * ML_HEADS, MLA_RANK, MLA_RANK, MLA_ROPE):
        idx.append(idx[-1] + wdt)
    seg = [w_in[..., a:b] for a, b in zip(idx[:-1], idx[1:])]
    s_gq, s_gk, s_gv, s_mq, s_mk, s_mv, s_mo, s_mg, s_ql, s_kv, s_kr = seg
    kr_slot_w = jnp.pad(s_kr, [(0, 0), (0, 0), (MLA_NOPE, LANES - MLA_QK)])
    nh = ML_HEADS
    gi_slot_w = _pad_last(jnp.concatenate([s_mg[..., 0:nh], s_mg[..., 2 * nh:3 * nh]], axis=-1), LANES)
    gf_slot_w = _pad_last(jnp.concatenate([s_mg[..., nh:2 * nh], s_mg[..., 3 * nh:4 * nh]], axis=-1), LANES)
    w_in_p = jnp.concatenate(
        [_pad_heads(s_gq, GQA_HEADS, HEAD_DIM), _pad_heads(s_gk, GQA_KV_HEADS, HEAD_DIM), s_gv,
         s_mk, s_mo, s_ql, s_kv, kr_slot_w, gi_slot_w, gf_slot_w], axis=-1).astype(BF16)
    assert w_in_p.shape[-1] == D_INP
    wmt = jnp.swapaxes(jnp.concatenate([s_mq, s_mk, s_mv], axis=-1), 1, 2).astype(BF16)
    wq = _pad_heads(mla_w_uq, MLA_HEADS, MLA_QK).astype(BF16)
    ukv = mla_w_ukv.reshape(depth, MLA_RANK, MLA_HEADS, MLA_NOPE + MLA_V)
    wk = _pad_last(ukv[..., :MLA_NOPE], LANES).reshape(depth, MLA_RANK, MLA_HEADS * LANES).astype(BF16)
    wvt = jnp.swapaxes(ukv[..., MLA_NOPE:].reshape(depth, MLA_RANK, MLA_HEADS * MLA_V), 1, 2).astype(BF16)
    wgvt = jnp.swapaxes(s_gv, 1, 2).astype(BF16)
    w_out_b = w_out.astype(BF16)
    qn = _pad_last(gqa_q_norm, LANES).reshape(depth, 1, LANES)
    kn = _pad_last(gqa_k_norm, LANES).reshape(depth, 1, LANES)
    gbi = _pad_last(jnp.concatenate([mlstm_gate_b[:, 0], mlstm_gate_b[:, 2]], axis=-1), LANES).reshape(depth, 1, LANES)
    gbf = _pad_last(jnp.concatenate([mlstm_gate_b[:, 1], mlstm_gate_b[:, 3]], axis=-1), LANES).reshape(depth, 1, LANES)
    mqn = mla_q_norm.reshape(depth, 1, MLA_RANK)
    mkvn = mla_kv_norm.reshape(depth, 1, MLA_RANK)
    onw = mlstm_out_norm.reshape(depth, 1, ML_W)
    norm_w4 = norm_w.reshape(depth, 3, 1, d)
    head_id = jnp.arange(ML_W) // ML_DK
    seg_mean = ((head_id[:, None] == head_id[None, :]).astype(F32) / ML_DK).astype(BF16)
    rope_tabs = _rope_tables(tl)

    ck_gqa = _pad_heads(cache_gqa_k.reshape(bl, depth, past, GQA_KV_HEADS * HEAD_DIM), GQA_KV_HEADS,
                        HEAD_DIM).astype(BF16)
    cvt_gqa = jnp.transpose(cache_gqa_v.reshape(bl * depth, past, GQA_KV_HEADS * HEAD_DIM), (2, 0, 1)).reshape(
        GQA_KV_HEADS * HEAD_DIM, bl, depth, past).astype(BF16)
    ckr_slot = jnp.pad(cache_mla_krope, [(0, 0), (0, 0), (0, 0), (MLA_NOPE, LANES - MLA_QK)])
    m0_lanes = jnp.stack([jnp.pad(state_mlstm_m[:, :, dd], [(0, 0), (0, 0), (dd * nh, LANES - (dd + 1) * nh)])
                          for dd in range(2)], axis=2)[:, :, :, None, :]

    nc_rows, nl_rows = bc * tc, bl * tl
    tiles = _plan_tiles(bc, tc, bl, tl)
    tm_c, tm_l, tm_ci = tiles.rows_c, tiles.rows_l, tiles.rows_ci
    tq_c, tq_l, nseq_c = tiles.tq_c, tiles.tq_l, tiles.nseq_c
    chunk_c = chunk_l = tiles.chunk
    nb_c, nb_l = tiles.nb_c, tiles.nb_l
    ctx_group = lambda i: 0
    lat_group = lambda i: 1 + i // (tl // tm_l)
    ffn_lat_group = lambda i: 1 + i // (tl // tiles.ffn_l)

    xc = x_prompt.reshape(nc_rows, d)
    xl = x_sample.reshape(nl_rows, d)
    collected = [[] for _ in range(7)]
    for l in range(depth):
        xc = _ffn(xc, mods, norm_w4, wg, wu, wd, l, 0, ctx_group, tiles.ffn_c)
        xl = _ffn(xl, mods, norm_w4, wg, wu, wd, l, 0, ffn_lat_group, tiles.ffn_l)

        (gq, gk, gv, mk, mqt, mkt, mvt, mo, gi, gf, qm, kp, vm, gkf, gvf, ckv, krs) = _inproj(
            xc, mods, norm_w4, w_in_p, wgvt, wmt, qn, kn, gbi, gbf, mqn, mkvn, wq, wk, wvt, None, l, ctx_group, tm_ci,
            tc, True)
        gqa_o = _attn(gq, [gk], [gv], bc, GQA_HEADS, GQA_GROUP, tq_c, nseq_c)
        mla_o = _attn(qm, [kp], [vm], bc, MLA_HEADS, 1, tq_c, nseq_c)
        hf, hb, c_new, n_new, m_lanes = _mlstm(mk, mqt, mkt, mvt, gi, gf, bc, chunk_c, nb_c, None, l, True)
        m_new = jnp.stack([m_lanes[:, dd, 0, dd * nh:(dd + 1) * nh] for dd in range(2)], axis=1)
        xc = _outproj(xc, mods, gqa_o, hf, hb, mo, mla_o, onw, seg_mean, w_out_b, l, ctx_group, tm_c)
        new_k = gkf.reshape(bc, tc, GQA_KV_HEADS, HEAD_DIM)
        new_v = gvf.reshape(bc, tc, GQA_KV_HEADS, HEAD_DIM)
        new_ckv = ckv.reshape(bc, tc, MLA_RANK)
        new_kr = krs.reshape(bc, tc, LANES)[..., MLA_NOPE:MLA_QK]
        for lst, t in zip(collected, (new_k, new_v, new_ckv, new_kr, c_new, n_new, m_new)):
            lst.append(t)

        (gq, gk, gv, mk, mqt, mkt, mvt, mo, gi, gf, qm, kp, vm) = _inproj(
            xl, mods, norm_w4, w_in_p, wgvt, wmt, qn, kn, gbi, gbf, mqn, mkvn, wq, wk, wvt, rope_tabs, l, lat_group,
            tm_l, tl, False)
        kp_c, vm_c = _kvexp(cache_mla_ckv[:, l].reshape(bl * past, MLA_RANK), ckr_slot[:, l].reshape(bl * past, LANES),
                            wk, wvt, l)
        gqa_o = _attn(gq, [ck_gqa[:, l].reshape(bl * past, -1), gk], [cvt_gqa[:, :, l].reshape(-1, bl * past), gv],
                      bl, GQA_HEADS, GQA_GROUP, tq_l, 1)
        mla_o = _attn(qm, [kp_c, kp], [vm_c, vm], bl, MLA_HEADS, 1, tq_l, 1)
        hf, hb = _mlstm(mk, mqt, mkt, mvt, gi, gf, bl, chunk_l, nb_l, (state_mlstm_C, state_mlstm_n, m0_lanes), l,
                        False)
        xl = _outproj(xl, mods, gqa_o, hf, hb, mo, mla_o, onw, seg_mean, w_out_b, l, lat_group, tm_l)

        fw = final_norm if l == depth - 1 else None
        xc = _ffn(xc, mods, norm_w4, wg, wu, wd, l, 1, ctx_group, tiles.ffn_c, fw)
        xl = _ffn(xl, mods, norm_w4, wg, wu, wd, l, 1, ffn_lat_group, tiles.ffn_l, fw)

    y_prompt = xc.reshape(bc, tc, d)
    y_sample = xl.reshape(bl, tl, d)
    stacked = [jnp.stack(lst, axis=1) for lst in collected]
    return (y_prompt, y_sample, *stacked)
```
